```python
import jax, jax.numpy as jnp
from jax import lax
import numpy as np

D_MODEL = 1024
BATCH = 32
SEQ = 2048
DEPTH = 4

GRID_W = 64
CTX_LEN = 256
HEAD_DIM = 64
ATTN_W = D_MODEL // 2
N_Q_HEADS = ATTN_W // HEAD_DIM
N_KV_HEADS = N_Q_HEADS // 4
Q_PER_KV = N_Q_HEADS // N_KV_HEADS
KV_W = N_KV_HEADS * HEAD_DIM
POOL_W = D_MODEL // 4
POOL_WINDOWS = (2, 4, 8, 16)
N_POOL_GROUPS = len(POOL_WINDOWS)
POOL_GROUP_W = POOL_W // N_POOL_GROUPS
FFT_W = D_MODEL // 4
N_FFT_GROUPS = 4
FFT_GROUP_W = FFT_W // N_FFT_GROUPS
MIX_W = ATTN_W + POOL_W + FFT_W
IN_W = ATTN_W + 2 * KV_W + POOL_W + FFT_W
D_FF = ((8 * D_MODEL // 3 + 127) // 128) * 128
ROPE_THETA = 10000.0
Q_BLOCK = 128
N_MOD = 9
EPS = 1e-6
ATTN_SCALE = HEAD_DIM ** -0.5

kernel_name = 'hybrid_attn_pool_fourier_dit_block'


def rms_norm(x):
    xf = x.astype(jnp.float32)
    return (xf * lax.rsqrt(jnp.mean(xf * xf, axis=-1, keepdims=True) + EPS)).astype(x.dtype)


def modulate(h, shift, scale):
    return h * (1 + scale[:, None, :]) + shift[:, None, :]


def swiglu(h, w_gu, w_down):
    g, u = jnp.split(h @ w_gu, 2, axis=-1)
    return (jax.nn.silu(g) * u) @ w_down


def axial_rope_tables(length):
    rows = length // GRID_W
    row = jnp.repeat(jnp.arange(rows), GRID_W).astype(jnp.float32)
    col = jnp.tile(jnp.arange(GRID_W), rows).astype(jnp.float32)
    n_freq = HEAD_DIM // 4
    inv_freq = ROPE_THETA ** (-jnp.arange(n_freq, dtype=jnp.float32) / n_freq)
    ang = jnp.concatenate([row[:, None] * inv_freq, col[:, None] * inv_freq], axis=-1)
    return jnp.cos(ang), jnp.sin(ang)


def apply_rope(x, cos, sin):
    shape = (1, cos.shape[0]) + (1,) * (x.ndim - 3) + (cos.shape[1],)
    cos = cos.reshape(shape).astype(x.dtype)
    sin = sin.reshape(shape).astype(x.dtype)
    half = HEAD_DIM // 2
    x1, x2 = x[..., :half], x[..., half:]
    return jnp.concatenate([x1 * cos - x2 * sin, x1 * sin + x2 * cos], axis=-1)


def split_in(z):
    b = [ATTN_W, ATTN_W + KV_W, ATTN_W + 2 * KV_W, ATTN_W + 2 * KV_W + POOL_W]
    return jnp.split(z, b, axis=-1)


def q_heads(q, gain):
    q = q.reshape(q.shape[:2] + (N_KV_HEADS, Q_PER_KV, HEAD_DIM))
    return rms_norm(q) * gain


def kv_heads(t):
    return t.reshape(t.shape[:2] + (N_KV_HEADS, HEAD_DIM))


def attend_latent(q, k, v, k_ctx, v_ctx):
    k_all = jnp.concatenate([k, k_ctx], axis=1)
    v_all = jnp.concatenate([v, v_ctx], axis=1)
    bsz, length = q.shape[:2]
    n_blk = length // Q_BLOCK
    qb = q.reshape((bsz, n_blk, Q_BLOCK) + q.shape[2:]).transpose(1, 0, 2, 3, 4, 5)

    def one_block(q_blk):
        s = jnp.einsum('bqkgd,bskd->bkgqs', q_blk, k_all).astype(jnp.float32) * ATTN_SCALE
        p = jax.nn.softmax(s, axis=-1).astype(v_all.dtype)
        return jnp.einsum('bkgqs,bskd->bqkgd', p, v_all)

    o = lax.map(one_block, qb)
    return o.transpose(1, 0, 2, 3, 4, 5).reshape(bsz, length, ATTN_W)


def attend_context(q, k, v):
    s = jnp.einsum('bqkgd,bskd->bkgqs', q, k).astype(jnp.float32) * ATTN_SCALE
    p = jax.nn.softmax(s, axis=-1).astype(v.dtype)
    o = jnp.einsum('bkgqs,bskd->bqkgd', p, v)
    return o.reshape(q.shape[0], q.shape[1], ATTN_W)


def pool_minus_self(u):
    bsz, length, _ = u.shape
    ug = u.reshape(bsz, length, N_POOL_GROUPS, POOL_GROUP_W).astype(jnp.float32)
    cs = jnp.cumsum(ug, axis=1)
    cs = jnp.concatenate([jnp.zeros_like(cs[:, :1]), cs], axis=1)
    t = jnp.arange(length)
    means = []
    for g, w in enumerate(POOL_WINDOWS):
        lo = jnp.clip(t - w // 2, 0, length)
        hi = jnp.clip(t + w - w // 2, 0, length)
        cs_g = cs[:, :, g]
        s = jnp.take(cs_g, hi, axis=1) - jnp.take(cs_g, lo, axis=1)
        cnt = (hi - lo).astype(jnp.float32)[None, :, None]
        means.append(s / cnt)
    mean = jnp.stack(means, axis=2)
    return (mean - ug).astype(u.dtype)


def fourier_2d(u):
    bsz, length, _ = u.shape
    ug = u.reshape(bsz, length, N_FFT_GROUPS, FFT_GROUP_W).astype(jnp.float32)
    f = jnp.fft.fft2(ug, axes=(1, 3), norm='ortho').real
    return f.reshape(bsz, length, FFT_W).astype(u.dtype)


def merge_mixers(attn_out, u_pool, u_fft, pool_w_l, pool_scale_l, fft_w_l, w_out_l):
    bsz, length, _ = u_pool.shape
    yp = jnp.einsum('blgc,gcd->blgd', pool_minus_self(u_pool), pool_w_l).reshape(bsz, length, POOL_W) * pool_scale_l
    yf = fourier_2d(u_fft) @ fft_w_l
    return jnp.concatenate([attn_out, yp, yf], axis=-1) @ w_out_l


def setup_inputs(seed: int = 0) -> dict:
    key = jax.random.key(seed)
    ks = jax.random.split(key, 20)

    def nrm(k, shape, std):
        return jax.random.normal(k, shape, jnp.float32) * std

    return {
        'x': nrm(ks[0], (BATCH, SEQ, D_MODEL), 1.0),
        'c': nrm(ks[1], (BATCH, D_MODEL), 1.0),
        'ctx': nrm(ks[2], (BATCH, CTX_LEN, D_MODEL), 1.0),
        'c_ctx': nrm(ks[3], (D_MODEL,), 1.0),
        'ada_w': nrm(ks[4], (DEPTH, D_MODEL, N_MOD * D_MODEL), 0.5 * D_MODEL ** -0.5),
        'ada_b': nrm(ks[5], (DEPTH, N_MOD * D_MODEL), 0.01),
        'ffn1_w_gu': nrm(ks[6], (DEPTH, D_MODEL, 2 * D_FF), D_MODEL ** -0.5),
        'ffn1_w_down': nrm(ks[7], (DEPTH, D_FF, D_MODEL), D_FF ** -0.5),
        'w_in': nrm(ks[8], (DEPTH, D_MODEL, IN_W), D_MODEL ** -0.5),
        'q_gain': 1.0 + nrm(ks[9], (DEPTH, HEAD_DIM), 0.1),
        'k_gain': 1.0 + nrm(ks[10], (DEPTH, HEAD_DIM), 0.1),
        'pool_w': nrm(ks[11], (DEPTH, N_POOL_GROUPS, POOL_GROUP_W, POOL_GROUP_W), POOL_GROUP_W ** -0.5),
        'pool_scale': 1.0 + nrm(ks[12], (DEPTH, POOL_W), 0.1),
        'fft_w': nrm(ks[13], (DEPTH, FFT_W, FFT_W), FFT_W ** -0.5),
        'w_out': nrm(ks[14], (DEPTH, MIX_W, D_MODEL), MIX_W ** -0.5),
        'ffn2_w_gu': nrm(ks[15], (DEPTH, D_MODEL, 2 * D_FF), D_MODEL ** -0.5),
        'ffn2_w_down': nrm(ks[16], (DEPTH, D_FF, D_MODEL), D_FF ** -0.5),
        'final_gain': 1.0 + nrm(ks[17], (D_MODEL,), 0.1),
    }


def reference(x, c, ctx, c_ctx, ada_w, ada_b, ffn1_w_gu, ffn1_w_down, w_in, q_gain, k_gain,
              pool_w, pool_scale, fft_w, w_out, ffn2_w_gu, ffn2_w_down, final_gain):
    cos, sin = axial_rope_tables(x.shape[1])
    for layer in range(DEPTH):
        last = layer == DEPTH - 1
        mx = jnp.split(jax.nn.silu(c) @ ada_w[layer] + ada_b[layer], N_MOD, axis=-1)
        mc = jnp.split((jax.nn.silu(c_ctx) @ ada_w[layer] + ada_b[layer])[None, :], N_MOD, axis=-1)

        x = x + 0.5 * mx[2][:, None] * swiglu(modulate(rms_norm(x), mx[0], mx[1]), ffn1_w_gu[layer], ffn1_w_down[layer])
        ctx = ctx + 0.5 * mc[2][:, None] * swiglu(modulate(rms_norm(ctx), mc[0], mc[1]), ffn1_w_gu[layer], ffn1_w_down[layer])

        hx = modulate(rms_norm(x), mx[3], mx[4])
        hc = modulate(rms_norm(ctx), mc[3], mc[4])
        if last:
            kc, vc = jnp.split(hc @ w_in[layer, :, ATTN_W:ATTN_W + 2 * KV_W], 2, axis=-1)
        else:
            qc, kc, vc, upc, ufc = split_in(hc @ w_in[layer])
        kc = rms_norm(kv_heads(kc)) * k_gain[layer]
        vc = kv_heads(vc)

        qx, kx, vx, upx, ufx = split_in(hx @ w_in[layer])
        qx = apply_rope(q_heads(qx, q_gain[layer]), cos, sin)
        kx = apply_rope(rms_norm(kv_heads(kx)) * k_gain[layer], cos, sin)
        vx = kv_heads(vx)
        ax = attend_latent(qx, kx, vx, kc, vc)
        x = x + mx[5][:, None] * merge_mixers(ax, upx, ufx, pool_w[layer], pool_scale[layer], fft_w[layer], w_out[layer])

        x = x + 0.5 * mx[8][:, None] * swiglu(modulate(rms_norm(x), mx[6], mx[7]), ffn2_w_gu[layer], ffn2_w_down[layer])

        if not last:
            ac = attend_context(q_heads(qc, q_gain[layer]), kc, vc)
            ctx = ctx + mc[5][:, None] * merge_mixers(ac, upc, ufc, pool_w[layer], pool_scale[layer], fft_w[layer], w_out[layer])
            ctx = ctx + 0.5 * mc[8][:, None] * swiglu(modulate(rms_norm(ctx), mc[6], mc[7]), ffn2_w_gu[layer], ffn2_w_down[layer])

    return rms_norm(x) * final_gain
```

```python
import functools
import math

import jax
import jax.numpy as jnp
from jax import lax
from jax.experimental import pallas as pl
from jax.experimental.pallas import tpu as pltpu

F32 = jnp.float32
BF16 = jnp.bfloat16

GRID_W = 64
HEAD_DIM = 64
HALF = HEAD_DIM // 2
N_Q_HEADS = 8
N_KV_HEADS = 2
Q_PER_KV = N_Q_HEADS // N_KV_HEADS
ATTN_W = N_Q_HEADS * HEAD_DIM
KV_W = N_KV_HEADS * HEAD_DIM
QKV_W = ATTN_W + 2 * KV_W
POOL_W = 256
POOL_HALF_WINDOWS = (1, 2, 4, 8)
POOL_GROUP_W = POOL_W // len(POOL_HALF_WINDOWS)
POOL_PAD = 8
FFT_W = 256
N_FFT_GROUPS = 4
FFT_GROUP_W = FFT_W // N_FFT_GROUPS
N_MOD = 9
ROPE_THETA = 10000.0
EPS = 1e-6
Q_SCALE = HEAD_DIM ** -0.5 * math.log2(math.e)

V7X_VMEM_LIMIT = 56 * 1024 * 1024
ROW_TILE = 512
Q_TILE = 256


def _dot(a, b):
    return jnp.dot(a, b, preferred_element_type=F32)


def _dot_nt(a, b):
    return lax.dot_general(a, b, (((1,), (1,)), ((), ())), preferred_element_type=F32)


def _dot_tn(a, b):
    return lax.dot_general(a, b, (((0,), (0,)), ((), ())), preferred_element_type=F32)


def _rms(x):
    return x * lax.rsqrt(jnp.mean(x * x, axis=-1, keepdims=True) + EPS)


def _resident(shape, index_map):
    return pl.BlockSpec(shape, index_map, pipeline_mode=pl.Buffered(1))


def _params(n_axes):
    return pltpu.CompilerParams(dimension_semantics=("arbitrary",) * n_axes,
                                vmem_limit_bytes=V7X_VMEM_LIMIT)


def _ada_kernel(cc_ref, w_ref, b_ref, o_ref):
    a = cc_ref[...]
    a = a / (1.0 + jnp.exp(-a))
    a_hi = a.astype(BF16)
    a_lo = (a - a_hi.astype(F32)).astype(BF16)
    w = w_ref[0]
    w_hi = w.astype(BF16)
    w_lo = (w - w_hi.astype(F32)).astype(BF16)
    acc = _dot(a_hi, w_hi) + _dot(a_lo, w_hi) + _dot(a_hi, w_lo)
    o_ref[0, 0] = acc + b_ref[0]


def _ada(cc, ada_w, ada_b):
    depth, d, _ = ada_w.shape
    rows = cc.shape[0]
    return pl.pallas_call(
        _ada_kernel,
        out_shape=jax.ShapeDtypeStruct((depth, N_MOD, rows, d), F32),
        grid=(depth, N_MOD),
        in_specs=[pl.BlockSpec((rows, d), lambda l, n: (0, 0)),
                  pl.BlockSpec((1, d, d), lambda l, n: (l, 0, n)),
                  pl.BlockSpec((1, 1, d), lambda l, n: (l, 0, n))],
        out_specs=pl.BlockSpec((1, 1, rows, d), lambda l, n: (l, n, 0, 0)),
        compiler_params=_params(2),
        name="ada_mod",
    )(cc, ada_w, ada_b.reshape(depth, 1, N_MOD * d))


def _mod_spec(k, per_batch, ctx_row):
    if per_batch:
        return lambda d: pl.BlockSpec((1, 1, 1, 1, d), lambda b, i, l: (l[0], k, b, 0, 0))
    return lambda d: pl.BlockSpec((1, 1, 1, 1, d), lambda b, i, l: (l[0], k, ctx_row, 0, 0))


def _ffn_kernel(l_ref, x_ref, shift_ref, scale_ref, gate_ref, wgu_ref, wd_ref, fg_ref, o_ref, *, final):
    x = x_ref[0]
    h = (_rms(x) * (1.0 + scale_ref[0, 0, 0]) + shift_ref[0, 0, 0]).astype(BF16)
    gu = _dot(h, wgu_ref[0])
    f = gu.shape[1] // 2
    g = gu[:, :f]
    a = (g / (1.0 + jnp.exp(-g)) * gu[:, f:]).astype(BF16)
    y = _dot(a, wd_ref[0])
    out = x + (0.5 * gate_ref[0, 0, 0]) * y
    if final:
        out = _rms(out) * fg_ref[...]
    o_ref[0] = out


def _ffn(l_arr, x, mod, mod_k, per_batch, ctx_row, wgu, wd, final_gain, final):
    bx, rows, d = x.shape
    tm = min(ROW_TILE, rows)
    f2 = wgu.shape[2]
    specs = [_mod_spec(k, per_batch, ctx_row)(d) for k in mod_k]
    return pl.pallas_call(
        functools.partial(_ffn_kernel, final=final),
        out_shape=jax.ShapeDtypeStruct(x.shape, F32),
        grid_spec=pltpu.PrefetchScalarGridSpec(
            num_scalar_prefetch=1,
            grid=(bx, rows // tm),
            in_specs=[pl.BlockSpec((1, tm, d), lambda b, i, l: (b, i, 0))] + specs + [
                _resident((1, d, f2), lambda b, i, l: (l[0], 0, 0)),
                _resident((1, f2 // 2, d), lambda b, i, l: (l[0], 0, 0)),
                pl.BlockSpec((1, d), lambda b, i, l: (0, 0))],
            out_specs=pl.BlockSpec((1, tm, d), lambda b, i, l: (b, i, 0))),
        compiler_params=_params(2),
        name="ffn_final" if final else "ffn",
    )(l_arr, x, mod, mod, mod, wgu, wd, final_gain)


def _inproj_kernel(l_ref, x_ref, shift_ref, scale_ref, wqkv_ref, wpf_ref, qg_ref, kg_ref, cos_ref, sin_ref,
                   dftc_ref, qT_ref, k_ref, vT_ref, up_ref, xcs_ref, *, rope):
    x = x_ref[0]
    h = (_rms(x) * (1.0 + scale_ref[0, 0, 0]) + shift_ref[0, 0, 0]).astype(BF16)
    zT = _dot_nt(wqkv_ref[0], h)
    z2 = _dot(h, wpf_ref[0])

    def head(zh, gain):
        zn = zh * lax.rsqrt(jnp.mean(zh * zh, axis=0, keepdims=True) + EPS) * gain
        x1, x2 = zn[:HALF], zn[HALF:]
        if not rope:
            return x1, x2
        c, s = cos_ref[...], sin_ref[...]
        return x1 * c - x2 * s, x1 * s + x2 * c

    for j in range(N_Q_HEADS):
        r = j * HEAD_DIM
        a, b = head(zT[r:r + HEAD_DIM], qg_ref[0])
        qT_ref[0, r:r + HALF, :] = (a * Q_SCALE).astype(BF16)
        qT_ref[0, r + HALF:r + HEAD_DIM, :] = (b * Q_SCALE).astype(BF16)
    parts = []
    for j in range(N_KV_HEADS):
        r = ATTN_W + j * HEAD_DIM
        parts.extend(head(zT[r:r + HEAD_DIM], kg_ref[0]))
    k_ref[0] = jnp.concatenate(parts, axis=0).T.astype(BF16)
    vT_ref[0] = zT[ATTN_W + KV_W:].astype(BF16)
    up_ref[0] = z2[:, :POOL_W]
    xcs_ref[0] = _dot(z2[:, POOL_W:].astype(BF16), dftc_ref[...]).astype(BF16)


def _inproj(l_arr, x, mod, per_batch, ctx_row, wqkvT, wpf, qg, kg, cosT, sinT, dftc, rope):
    b_, rows, d = x.shape
    tm = min(ROW_TILE, rows)
    pf = POOL_W + FFT_W
    specs = [_mod_spec(k, per_batch, ctx_row)(d) for k in (3, 4)]
    out_shape = (jax.ShapeDtypeStruct((b_, ATTN_W, rows), BF16),
                 jax.ShapeDtypeStruct((b_, rows, KV_W), BF16),
                 jax.ShapeDtypeStruct((b_, KV_W, rows), BF16),
                 jax.ShapeDtypeStruct((b_, rows, POOL_W), F32),
                 jax.ShapeDtypeStruct((b_, rows, 2 * FFT_W), BF16))
    return pl.pallas_call(
        functools.partial(_inproj_kernel, rope=rope),
        out_shape=out_shape,
        grid_spec=pltpu.PrefetchScalarGridSpec(
            num_scalar_prefetch=1,
            grid=(b_, rows // tm),
            in_specs=[pl.BlockSpec((1, tm, d), lambda b, i, l: (b, i, 0))] + specs + [
                pl.BlockSpec((1, QKV_W, d), lambda b, i, l: (l[0], 0, 0)),
                pl.BlockSpec((1, d, pf), lambda b, i, l: (l[0], 0, 0)),
                pl.BlockSpec((1, HEAD_DIM, 1), lambda b, i, l: (l[0], 0, 0)),
                pl.BlockSpec((1, HEAD_DIM, 1), lambda b, i, l: (l[0], 0, 0)),
                pl.BlockSpec((HALF, tm), lambda b, i, l: (0, i)),
                pl.BlockSpec((HALF, tm), lambda b, i, l: (0, i)),
                pl.BlockSpec((FFT_W, 2 * FFT_W), lambda b, i, l: (0, 0))],
            out_specs=[pl.BlockSpec((1, ATTN_W, tm), lambda b, i, l: (b, 0, i)),
                       pl.BlockSpec((1, tm, KV_W), lambda b, i, l: (b, i, 0)),
                       pl.BlockSpec((1, KV_W, tm), lambda b, i, l: (b, 0, i)),
                       pl.BlockSpec((1, tm, POOL_W), lambda b, i, l: (b, i, 0)),
                       pl.BlockSpec((1, tm, 2 * FFT_W), lambda b, i, l: (b, i, 0))]),
        compiler_params=_params(2),
        name="in_proj_rope" if rope else "in_proj",
    )(l_arr, x, mod, mod, wqkvT, wpf, qg, kg, cosT, sinT, dftc)


def _attn_kernel(*refs, nseg):
    qT_ref = refs[0]
    k_refs = refs[1:1 + nseg]
    v_refs = refs[1 + nseg:1 + 2 * nseg]
    o_ref = refs[1 + 2 * nseg]
    for j in range(N_Q_HEADS):
        kvh = j // Q_PER_KV
        r = j * HEAD_DIM
        qj = qT_ref[0, r:r + HEAD_DIM, :]
        zero = jnp.zeros_like(qj)
        qb = jnp.concatenate([qj, zero] if kvh == 0 else [zero, qj], axis=0)
        ss = [_dot(k[0], qb) for k in k_refs]
        m = functools.reduce(jnp.maximum, [jnp.max(s, axis=0, keepdims=True) for s in ss])
        ps = [jnp.exp2(s - m) for s in ss]
        den = functools.reduce(jnp.add, [jnp.sum(p, axis=0, keepdims=True) for p in ps])
        acc = functools.reduce(jnp.add, [_dot(v[0], p.astype(BF16)) for v, p in zip(v_refs, ps)])
        o = acc[kvh * HEAD_DIM:(kvh + 1) * HEAD_DIM] / den
        o_ref[0, r:r + HEAD_DIM, :] = o.astype(BF16)


def _attn(qT, ks, vTs):
    b_, _, rows = qT.shape
    tq = min(Q_TILE, rows)
    nseg = len(ks)
    k_specs = [pl.BlockSpec((1, k.shape[1], KV_W), lambda b, i: (b, 0, 0)) for k in ks]
    v_specs = [pl.BlockSpec((1, KV_W, v.shape[2]), lambda b, i: (b, 0, 0)) for v in vTs]
    return pl.pallas_call(
        functools.partial(_attn_kernel, nseg=nseg),
        out_shape=jax.ShapeDtypeStruct(qT.shape, BF16),
        grid=(b_, rows // tq),
        in_specs=[pl.BlockSpec((1, ATTN_W, tq), lambda b, i: (b, 0, i))] + k_specs + v_specs,
        out_specs=pl.BlockSpec((1, ATTN_W, tq), lambda b, i: (b, 0, i)),
        compiler_params=_params(2),
        name="attn%d" % nseg,
    )(qT, *ks, *vTs)


def _mix_kernel(l_ref, x_ref, gate_ref, oT_ref, up_ref, xcs_ref, cl_ref, nsl_ref, fftw_ref, poolw_ref, pscale_ref,
                wout_ref, o_ref, pad_ref, pm_ref, *, seq, tm):
    i = pl.program_id(1)

    @pl.when(i == 0)
    def _():
        u = up_ref[0]
        zeros = jnp.zeros((POOL_PAD, POOL_W), F32)
        pad_ref[0:POOL_PAD, :] = zeros
        pad_ref[POOL_PAD + seq:2 * POOL_PAD + seq, :] = zeros
        pad_ref[POOL_PAD:POOL_PAD + seq, :] = u

        def shifted(o):
            return pad_ref[pl.ds(POOL_PAD + o, seq), :]

        sums = []
        acc = None
        prev = 0
        for hw in POOL_HALF_WINDOWS:
            for o in list(range(-hw, -prev)) + list(range(prev, hw)):
                acc = shifted(o) if acc is None else acc + shifted(o)
            sums.append(acc)
            prev = hw
        grp = lax.broadcasted_iota(jnp.int32, (seq, POOL_W), 1) // POOL_GROUP_W
        t = lax.broadcasted_iota(jnp.int32, (seq, POOL_W), 0)
        sel = sums[-1]
        hw_sel = jnp.full((seq, POOL_W), POOL_HALF_WINDOWS[-1], jnp.int32)
        for g in range(len(POOL_HALF_WINDOWS) - 2, -1, -1):
            sel = jnp.where(grp == g, sums[g], sel)
            hw_sel = jnp.where(grp == g, POOL_HALF_WINDOWS[g], hw_sel)
        cnt = (jnp.minimum(t + hw_sel, seq) - jnp.maximum(t - hw_sel, 0)).astype(F32)
        pm_ref[...] = (sel / cnt - u).astype(BF16)

    r0 = pl.multiple_of(i * tm, tm)
    f = (_dot(cl_ref[pl.ds(r0, tm), :], xcs_ref[0, :, 0:FFT_W])
         + _dot(nsl_ref[pl.ds(r0, tm), :], xcs_ref[0, :, FFT_W:2 * FFT_W]))
    yf = _dot(f.astype(BF16), fftw_ref[0])
    yp = _dot(pm_ref[pl.ds(r0, tm), :], poolw_ref[0]) * pscale_ref[0]
    y = (_dot_tn(oT_ref[0], wout_ref[0, 0:ATTN_W, :])
         + _dot(yp.astype(BF16), wout_ref[0, ATTN_W:ATTN_W + POOL_W, :])
         + _dot(yf.astype(BF16), wout_ref[0, ATTN_W + POOL_W:, :]))
    o_ref[0] = x_ref[0] + gate_ref[0, 0, 0] * y


def _mix(l_arr, x, mod, per_batch, ctx_row, oT, up, xcs, cl, nsl, fftw, poolbd, pscale, wout):
    b_, seq, d = x.shape
    tm = min(ROW_TILE, seq)
    return pl.pallas_call(
        functools.partial(_mix_kernel, seq=seq, tm=tm),
        out_shape=jax.ShapeDtypeStruct(x.shape, F32),
        grid_spec=pltpu.PrefetchScalarGridSpec(
            num_scalar_prefetch=1,
            grid=(b_, seq // tm),
            in_specs=[pl.BlockSpec((1, tm, d), lambda b, i, l: (b, i, 0)),
                      _mod_spec(5, per_batch, ctx_row)(d),
                      pl.BlockSpec((1, ATTN_W, tm), lambda b, i, l: (b, 0, i)),
                      pl.BlockSpec((1, seq, POOL_W), lambda b, i, l: (b, 0, 0)),
                      pl.BlockSpec((1, seq, 2 * FFT_W), lambda b, i, l: (b, 0, 0)),
                      _resident((seq, seq), lambda b, i, l: (0, 0)),
                      _resident((seq, seq), lambda b, i, l: (0, 0)),
                      pl.BlockSpec((1, FFT_W, FFT_W), lambda b, i, l: (l[0], 0, 0)),
                      pl.BlockSpec((1, POOL_W, POOL_W), lambda b, i, l: (l[0], 0, 0)),
                      pl.BlockSpec((1, 1, POOL_W), lambda b, i, l: (l[0], 0, 0)),
                      pl.BlockSpec((1, d, d), lambda b, i, l: (l[0], 0, 0))],
            out_specs=pl.BlockSpec((1, tm, d), lambda b, i, l: (b, i, 0)),
            scratch_shapes=[pltpu.VMEM((seq + 2 * POOL_PAD, POOL_W), F32),
                            pltpu.VMEM((seq, POOL_W), BF16)]),
        compiler_params=_params(2),
        name="mix",
    )(l_arr, x, mod, oT, up, xcs, cl, nsl, fftw, poolbd, pscale, wout)


def _rope_tables(length):
    t = jnp.arange(length)
    n_freq = HEAD_DIM // 4
    inv_freq = ROPE_THETA ** (-jnp.arange(n_freq, dtype=F32) / n_freq)
    row = (t // GRID_W).astype(F32)
    col = (t % GRID_W).astype(F32)
    ang = jnp.concatenate([inv_freq[:, None] * row[None, :], inv_freq[:, None] * col[None, :]], axis=0)
    return jnp.cos(ang), jnp.sin(ang)


def _dft(n):
    k = jnp.arange(n)
    ang = ((k[:, None] * k[None, :]) % n).astype(F32) * (2.0 * math.pi / n)
    return jnp.cos(ang) * n ** -0.5, jnp.sin(ang) * n ** -0.5


def _block_diag(blocks):
    g, w, _ = blocks.shape
    out = jnp.zeros((g * w, g * w), blocks.dtype)
    for i in range(g):
        out = out.at[i * w:(i + 1) * w, i * w:(i + 1) * w].set(blocks[i])
    return out


def kernel(x, c, ctx, c_ctx, ada_w, ada_b, ffn1_w_gu, ffn1_w_down, w_in, q_gain, k_gain, pool_w, pool_scale,
           fft_w, w_out, ffn2_w_gu, ffn2_w_down, final_gain):
    bsz, seq, d = x.shape
    clen = ctx.shape[1]
    depth = ada_w.shape[0]

    rows = -(-(bsz + 1) // 8) * 8
    cc = jnp.zeros((rows, d), F32).at[:bsz].set(c).at[bsz].set(c_ctx)
    mod = _ada(cc, ada_w, ada_b).reshape(depth, N_MOD, rows, 1, d)

    w1gu, w1d = ffn1_w_gu.astype(BF16), ffn1_w_down.astype(BF16)
    w2gu, w2d = ffn2_w_gu.astype(BF16), ffn2_w_down.astype(BF16)
    wqkvT = jnp.swapaxes(w_in[:, :, :QKV_W], 1, 2).astype(BF16)
    wpf = w_in[:, :, QKV_W:].astype(BF16)
    qg = q_gain.reshape(depth, HEAD_DIM, 1)
    kg = k_gain.reshape(depth, HEAD_DIM, 1)
    poolbd = jax.vmap(_block_diag)(pool_w).astype(BF16)
    pscale = pool_scale.reshape(depth, 1, POOL_W)
    fftw = fft_w.astype(BF16)
    wout = w_out.astype(BF16)
    fg = final_gain.reshape(1, d)

    cosT, sinT = _rope_tables(seq)
    ones = jnp.ones((HALF, clen), F32)
    cc_, sc_ = _dft(FFT_GROUP_W)
    dftc = jnp.concatenate([_block_diag(jnp.stack([cc_] * N_FFT_GROUPS)),
                            _block_diag(jnp.stack([sc_] * N_FFT_GROUPS))], axis=1).astype(BF16)
    cl_x, sl_x = _dft(seq)
    cl_c, sl_c = _dft(clen)
    cl_x, nsl_x = cl_x.astype(BF16), (-sl_x).astype(BF16)
    cl_c, nsl_c = cl_c.astype(BF16), (-sl_c).astype(BF16)

    def ffn_x(l_arr, xv, ks, wgu, wd, final=False):
        return _ffn(l_arr, xv, mod, ks, True, bsz, wgu, wd, fg, final)

    def ffn_c(l_arr, cv, ks, wgu, wd):
        flat = cv.reshape(1, bsz * clen, d)
        return _ffn(l_arr, flat, mod, ks, False, bsz, wgu, wd, fg, False).reshape(bsz, clen, d)

    def layer(l_arr, xv, cv, last):
        xv = ffn_x(l_arr, xv, (0, 1, 2), w1gu, w1d)
        cv = ffn_c(l_arr, cv, (0, 1, 2), w1gu, w1d)
        qT, k, vT, up, xcs = _inproj(l_arr, xv, mod, True, bsz, wqkvT, wpf, qg, kg, cosT, sinT, dftc, True)
        qcT, kc, vcT, upc, xcsc = _inproj(l_arr, cv, mod, False, bsz, wqkvT, wpf, qg, kg, ones, ones, dftc, False)
        oT = _attn(qT, [k, kc], [vT, vcT])
        xv = _mix(l_arr, xv, mod, True, bsz, oT, up, xcs, cl_x, nsl_x, fftw, poolbd, pscale, wout)
        xv = ffn_x(l_arr, xv, (6, 7, 8), w2gu, w2d, final=last)
        if not last:
            ocT = _attn(qcT, [kc], [vcT])
            cv = _mix(l_arr, cv, mod, False, bsz, ocT, upc, xcsc, cl_c, nsl_c, fftw, poolbd, pscale, wout)
            cv = ffn_c(l_arr, cv, (6, 7, 8), w2gu, w2d)
        return xv, cv

    def body(carry, l):
        xv, cv = layer(l.reshape(1), carry[0], carry[1], False)
        return (xv, cv), None

    (x, ctx), _ = lax.scan(body, (x, ctx), jnp.arange(depth - 1, dtype=jnp.int32))
    x, _ = layer(jnp.full((1,), depth - 1, jnp.int32), x, ctx, True)
    return x
```

```python
import functools
import math

import jax
import jax.numpy as jnp
from jax import lax
from jax.experimental import pallas as pl
from jax.experimental.pallas import tpu as pltpu

F32 = jnp.float32
BF16 = jnp.bfloat16

GRID_W = 64
HEAD_DIM = 64
HALF = HEAD_DIM // 2
N_Q_HEADS = 8
N_KV_HEADS = 2
Q_PER_KV = N_Q_HEADS // N_KV_HEADS
ATTN_W = N_Q_HEADS * HEAD_DIM
KV_W = N_KV_HEADS * HEAD_DIM
QKV_W = ATTN_W + 2 * KV_W
POOL_W = 256
POOL_HALF_WINDOWS = (1, 2, 4, 8)
POOL_GROUP_W = POOL_W // len(POOL_HALF_WINDOWS)
POOL_PAD = 8
FFT_W = 256
N_FFT_GROUPS = 4
FFT_GROUP_W = FFT_W // N_FFT_GROUPS
N_MOD = 9
ROPE_THETA = 10000.0
EPS = 1e-6
Q_SCALE = HEAD_DIM ** -0.5 * math.log2(math.e)

V7X_VMEM_LIMIT = 56 * 1024 * 1024
ROW_TILE = 512
Q_TILE = 512
KEY_CHUNK = 256
QK_LOOKAHEAD = 3


def _dot(a, b):
    return jnp.dot(a, b, preferred_element_type=F32)


def _dot_nt(a, b):
    return lax.dot_general(a, b, (((1,), (1,)), ((), ())), preferred_element_type=F32)


def _dot_tn(a, b):
    return lax.dot_general(a, b, (((0,), (0,)), ((), ())), preferred_element_type=F32)


def _rms(x):
    return x * lax.rsqrt(jnp.mean(x * x, axis=-1, keepdims=True) + EPS)


def _resident(shape, index_map):
    return pl.BlockSpec(shape, index_map, pipeline_mode=pl.Buffered(1))


def _params(n_axes):
    return pltpu.CompilerParams(dimension_semantics=("arbitrary",) * n_axes,
                                vmem_limit_bytes=V7X_VMEM_LIMIT)


def _ada_kernel(cc_ref, w_ref, b_ref, o_ref):
    a = cc_ref[...]
    a = a / (1.0 + jnp.exp(-a))
    a_hi = a.astype(BF16)
    a_lo = (a - a_hi.astype(F32)).astype(BF16)
    w = w_ref[0]
    w_hi = w.astype(BF16)
    w_lo = (w - w_hi.astype(F32)).astype(BF16)
    acc = _dot(a_hi, w_hi) + _dot(a_lo, w_hi) + _dot(a_hi, w_lo)
    o_ref[0, 0] = acc + b_ref[0]


def _ada(cc, ada_w, ada_b):
    depth, d, _ = ada_w.shape
    rows = cc.shape[0]
    return pl.pallas_call(
        _ada_kernel,
        out_shape=jax.ShapeDtypeStruct((depth, N_MOD, rows, d), F32),
        grid=(depth, N_MOD),
        in_specs=[pl.BlockSpec((rows, d), lambda l, n: (0, 0)),
                  pl.BlockSpec((1, d, d), lambda l, n: (l, 0, n)),
                  pl.BlockSpec((1, 1, d), lambda l, n: (l, 0, n))],
        out_specs=pl.BlockSpec((1, 1, rows, d), lambda l, n: (l, n, 0, 0)),
        compiler_params=_params(2),
        name="ada_mod",
    )(cc, ada_w, ada_b.reshape(depth, 1, N_MOD * d))


def _mod_spec(k, per_batch, ctx_row):
    if per_batch:
        return lambda d: pl.BlockSpec((1, 1, 1, 1, d), lambda b, i, l: (l[0], k, b, 0, 0))
    return lambda d: pl.BlockSpec((1, 1, 1, 1, d), lambda b, i, l: (l[0], k, ctx_row, 0, 0))


def _ffn_kernel(l_ref, x_ref, shift_ref, scale_ref, gate_ref, wgu_ref, wd_ref, fg_ref, o_ref, *, final):
    x = x_ref[0]
    h = (_rms(x) * (1.0 + scale_ref[0, 0, 0]) + shift_ref[0, 0, 0]).astype(BF16)
    gu = _dot(h, wgu_ref[0])
    f = gu.shape[1] // 2
    g = gu[:, :f]
    a = (g / (1.0 + jnp.exp(-g)) * gu[:, f:]).astype(BF16)
    y = _dot(a, wd_ref[0])
    out = x + (0.5 * gate_ref[0, 0, 0]) * y
    if final:
        out = _rms(out) * fg_ref[...]
    o_ref[0] = out


def _ffn(l_arr, x, mod, mod_k, per_batch, ctx_row, wgu, wd, final_gain, final):
    bx, rows, d = x.shape
    tm = min(ROW_TILE, rows)
    f2 = wgu.shape[2]
    specs = [_mod_spec(k, per_batch, ctx_row)(d) for k in mod_k]
    return pl.pallas_call(
        functools.partial(_ffn_kernel, final=final),
        out_shape=jax.ShapeDtypeStruct(x.shape, F32),
        grid_spec=pltpu.PrefetchScalarGridSpec(
            num_scalar_prefetch=1,
            grid=(bx, rows // tm),
            in_specs=[pl.BlockSpec((1, tm, d), lambda b, i, l: (b, i, 0))] + specs + [
                _resident((1, d, f2), lambda b, i, l: (l[0], 0, 0)),
                _resident((1, f2 // 2, d), lambda b, i, l: (l[0], 0, 0)),
                pl.BlockSpec((1, d), lambda b, i, l: (0, 0))],
            out_specs=pl.BlockSpec((1, tm, d), lambda b, i, l: (b, i, 0))),
        compiler_params=_params(2),
        name="ffn_final" if final else "ffn",
    )(l_arr, x, mod, mod, mod, wgu, wd, final_gain)


def _inproj_kernel(l_ref, x_ref, shift_ref, scale_ref, wqkv_ref, wpf_ref, qg_ref, kg_ref, cos_ref, sin_ref,
                   dftc_ref, qT_ref, k_ref, vT_ref, up_ref, xcs_ref, *, rope):
    x = x_ref[0]
    h = (_rms(x) * (1.0 + scale_ref[0, 0, 0]) + shift_ref[0, 0, 0]).astype(BF16)
    zT = _dot_nt(wqkv_ref[0], h)
    z2 = _dot(h, wpf_ref[0])

    def head(zh, gain):
        zn = zh * lax.rsqrt(jnp.mean(zh * zh, axis=0, keepdims=True) + EPS) * gain
        x1, x2 = zn[:HALF], zn[HALF:]
        if not rope:
            return x1, x2
        c, s = cos_ref[...], sin_ref[...]
        return x1 * c - x2 * s, x1 * s + x2 * c

    for j in range(N_Q_HEADS):
        r = j * HEAD_DIM
        a, b = head(zT[r:r + HEAD_DIM], qg_ref[0])
        qT_ref[0, r:r + HALF, :] = (a * Q_SCALE).astype(BF16)
        qT_ref[0, r + HALF:r + HEAD_DIM, :] = (b * Q_SCALE).astype(BF16)
    parts = []
    for j in range(N_KV_HEADS):
        r = ATTN_W + j * HEAD_DIM
        parts.extend(head(zT[r:r + HEAD_DIM], kg_ref[0]))
    k_ref[0] = jnp.concatenate(parts, axis=0).T.astype(BF16)
    vT_ref[0] = zT[ATTN_W + KV_W:].astype(BF16)
    up_ref[0] = z2[:, :POOL_W]
    xcs_ref[0] = _dot(z2[:, POOL_W:].astype(BF16), dftc_ref[...]).astype(BF16)


def _inproj(l_arr, x, mod, per_batch, ctx_row, wqkvT, wpf, qg, kg, cosT, sinT, dftc, rope):
    b_, rows, d = x.shape
    tm = min(ROW_TILE, rows)
    pf = POOL_W + FFT_W
    specs = [_mod_spec(k, per_batch, ctx_row)(d) for k in (3, 4)]
    out_shape = (jax.ShapeDtypeStruct((b_, ATTN_W, rows), BF16),
                 jax.ShapeDtypeStruct((b_, rows, KV_W), BF16),
                 jax.ShapeDtypeStruct((b_, KV_W, rows), BF16),
                 jax.ShapeDtypeStruct((b_, rows, POOL_W), F32),
                 jax.ShapeDtypeStruct((b_, rows, 2 * FFT_W), BF16))
    return pl.pallas_call(
        functools.partial(_inproj_kernel, rope=rope),
        out_shape=out_shape,
        grid_spec=pltpu.PrefetchScalarGridSpec(
            num_scalar_prefetch=1,
            grid=(b_, rows // tm),
            in_specs=[pl.BlockSpec((1, tm, d), lambda b, i, l: (b, i, 0))] + specs + [
                pl.BlockSpec((1, QKV_W, d), lambda b, i, l: (l[0], 0, 0)),
                pl.BlockSpec((1, d, pf), lambda b, i, l: (l[0], 0, 0)),
                pl.BlockSpec((1, HEAD_DIM, 1), lambda b, i, l: (l[0], 0, 0)),
                pl.BlockSpec((1, HEAD_DIM, 1), lambda b, i, l: (l[0], 0, 0)),
                pl.BlockSpec((HALF, tm), lambda b, i, l: (0, i)),
                pl.BlockSpec((HALF, tm), lambda b, i, l: (0, i)),
                pl.BlockSpec((FFT_W, 2 * FFT_W), lambda b, i, l: (0, 0))],
            out_specs=[pl.BlockSpec((1, ATTN_W, tm), lambda b, i, l: (b, 0, i)),
                       pl.BlockSpec((1, tm, KV_W), lambda b, i, l: (b, i, 0)),
                       pl.BlockSpec((1, KV_W, tm), lambda b, i, l: (b, 0, i)),
                       pl.BlockSpec((1, tm, POOL_W), lambda b, i, l: (b, i, 0)),
                       pl.BlockSpec((1, tm, 2 * FFT_W), lambda b, i, l: (b, i, 0))]),
        compiler_params=_params(2),
        name="in_proj_rope" if rope else "in_proj",
    )(l_arr, x, mod, mod, wqkvT, wpf, qg, kg, cosT, sinT, dftc)


def _attn_kernel(qT_ref, k_ref, vT_ref, o_ref, *, key_chunk):
    n_chunks = k_ref.shape[1] // key_chunk
    steps = [(j, c) for j in range(N_Q_HEADS) for c in range(n_chunks)]

    def scores(j, c):
        qj = qT_ref[0, j * HEAD_DIM:(j + 1) * HEAD_DIM, :]
        zero = jnp.zeros_like(qj)
        qb = jnp.concatenate([qj, zero] if j // Q_PER_KV == 0 else [zero, qj], axis=0)
        return _dot(k_ref[0, c * key_chunk:(c + 1) * key_chunk, :], qb)

    pending = [scores(*st) for st in steps[:QK_LOOKAHEAD]]
    for n, (j, c) in enumerate(steps):
        lo = (j // Q_PER_KV) * HEAD_DIM
        s = pending.pop(0)
        if n + QK_LOOKAHEAD < len(steps):
            pending.append(scores(*steps[n + QK_LOOKAHEAD]))
        cm = jnp.max(s, axis=0, keepdims=True)
        m_new = cm if c == 0 else jnp.maximum(m, cm)
        p = jnp.exp2(s - m_new)
        ps = jnp.sum(p, axis=0, keepdims=True)
        pv = _dot(vT_ref[0, :, c * key_chunk:(c + 1) * key_chunk], p.astype(BF16))[lo:lo + HEAD_DIM]
        if c == 0:
            den, acc = ps, pv
        else:
            alpha = jnp.exp2(m - m_new)
            den, acc = alpha * den + ps, alpha * acc + pv
        m = m_new
        if c == n_chunks - 1:
            o_ref[0, j * HEAD_DIM:(j + 1) * HEAD_DIM, :] = (acc / den).astype(BF16)


def _attn(qT, k, vT):
    b_, _, rows = qT.shape
    keys = k.shape[1]
    tq = min(Q_TILE, rows)
    return pl.pallas_call(
        functools.partial(_attn_kernel, key_chunk=min(KEY_CHUNK, keys)),
        out_shape=jax.ShapeDtypeStruct(qT.shape, BF16),
        grid=(b_, rows // tq),
        in_specs=[pl.BlockSpec((1, ATTN_W, tq), lambda b, i: (b, 0, i)),
                  pl.BlockSpec((1, keys, KV_W), lambda b, i: (b, 0, 0)),
                  pl.BlockSpec((1, KV_W, keys), lambda b, i: (b, 0, 0))],
        out_specs=pl.BlockSpec((1, ATTN_W, tq), lambda b, i: (b, 0, i)),
        compiler_params=_params(2),
        name="attn",
    )(qT, k, vT)


def _mix_kernel(l_ref, x_ref, gate_ref, oT_ref, up_ref, xcs_ref, cl_ref, nsl_ref, fftw_ref, poolw_ref, pscale_ref,
                wout_ref, o_ref, pad_ref, pm_ref, *, seq, tm):
    i = pl.program_id(1)

    @pl.when(i == 0)
    def _():
        u = up_ref[0]
        zeros = jnp.zeros((POOL_PAD, POOL_W), F32)
        pad_ref[0:POOL_PAD, :] = zeros
        pad_ref[POOL_PAD + seq:2 * POOL_PAD + seq, :] = zeros
        pad_ref[POOL_PAD:POOL_PAD + seq, :] = u

        def shifted(o):
            return pad_ref[pl.ds(POOL_PAD + o, seq), :]

        sums = []
        acc = None
        prev = 0
        for hw in POOL_HALF_WINDOWS:
            for o in list(range(-hw, -prev)) + list(range(prev, hw)):
                acc = shifted(o) if acc is None else acc + shifted(o)
            sums.append(acc)
            prev = hw
        grp = lax.broadcasted_iota(jnp.int32, (seq, POOL_W), 1) // POOL_GROUP_W
        t = lax.broadcasted_iota(jnp.int32, (seq, POOL_W), 0)
        sel = sums[-1]
        hw_sel = jnp.full((seq, POOL_W), POOL_HALF_WINDOWS[-1], jnp.int32)
        for g in range(len(POOL_HALF_WINDOWS) - 2, -1, -1):
            sel = jnp.where(grp == g, sums[g], sel)
            hw_sel = jnp.where(grp == g, POOL_HALF_WINDOWS[g], hw_sel)
        cnt = (jnp.minimum(t + hw_sel, seq) - jnp.maximum(t - hw_sel, 0)).astype(F32)
        pm_ref[...] = (sel / cnt - u).astype(BF16)

    r0 = pl.multiple_of(i * tm, tm)
    f = (_dot(cl_ref[pl.ds(r0, tm), :], xcs_ref[0, :, 0:FFT_W])
         + _dot(nsl_ref[pl.ds(r0, tm), :], xcs_ref[0, :, FFT_W:2 * FFT_W]))
    yf = _dot(f.astype(BF16), fftw_ref[0])
    yp = _dot(pm_ref[pl.ds(r0, tm), :], poolw_ref[0]) * pscale_ref[0]
    y = (_dot_tn(oT_ref[0], wout_ref[0, 0:ATTN_W, :])
         + _dot(yp.astype(BF16), wout_ref[0, ATTN_W:ATTN_W + POOL_W, :])
         + _dot(yf.astype(BF16), wout_ref[0, ATTN_W + POOL_W:, :]))
    o_ref[0] = x_ref[0] + gate_ref[0, 0, 0] * y


def _mix(l_arr, x, mod, per_batch, ctx_row, oT, up, xcs, cl, nsl, fftw, poolbd, pscale, wout):
    b_, seq, d = x.shape
    tm = min(ROW_TILE, seq)
    return pl.pallas_call(
        functools.partial(_mix_kernel, seq=seq, tm=tm),
        out_shape=jax.ShapeDtypeStruct(x.shape, F32),
        grid_spec=pltpu.PrefetchScalarGridSpec(
            num_scalar_prefetch=1,
            grid=(b_, seq // tm),
            in_specs=[pl.BlockSpec((1, tm, d), lambda b, i, l: (b, i, 0)),
                      _mod_spec(5, per_batch, ctx_row)(d),
                      pl.BlockSpec((1, ATTN_W, tm), lambda b, i, l: (b, 0, i)),
                      pl.BlockSpec((1, seq, POOL_W), lambda b, i, l: (b, 0, 0)),
                      pl.BlockSpec((1, seq, 2 * FFT_W), lambda b, i, l: (b, 0, 0)),
                      _resident((seq, seq), lambda b, i, l: (0, 0)),
                      _resident((seq, seq), lambda b, i, l: (0, 0)),
                      pl.BlockSpec((1, FFT_W, FFT_W), lambda b, i, l: (l[0], 0, 0)),
                      pl.BlockSpec((1, POOL_W, POOL_W), lambda b, i, l: (l[0], 0, 0)),
                      pl.BlockSpec((1, 1, POOL_W), lambda b, i, l: (l[0], 0, 0)),
                      pl.BlockSpec((1, d, d), lambda b, i, l: (l[0], 0, 0))],
            out_specs=pl.BlockSpec((1, tm, d), lambda b, i, l: (b, i, 0)),
            scratch_shapes=[pltpu.VMEM((seq + 2 * POOL_PAD, POOL_W), F32),
                            pltpu.VMEM((seq, POOL_W), BF16)]),
        compiler_params=_params(2),
        name="mix",
    )(l_arr, x, mod, oT, up, xcs, cl, nsl, fftw, poolbd, pscale, wout)


def _rope_tables(length):
    t = jnp.arange(length)
    n_freq = HEAD_DIM // 4
    inv_freq = ROPE_THETA ** (-jnp.arange(n_freq, dtype=F32) / n_freq)
    row = (t // GRID_W).astype(F32)
    col = (t % GRID_W).astype(F32)
    ang = jnp.concatenate([inv_freq[:, None] * row[None, :], inv_freq[:, None] * col[None, :]], axis=0)
    return jnp.cos(ang), jnp.sin(ang)


def _dft(n):
    k = jnp.arange(n)
    ang = ((k[:, None] * k[None, :]) % n).astype(F32) * (2.0 * math.pi / n)
    return jnp.cos(ang) * n ** -0.5, jnp.sin(ang) * n ** -0.5


def _block_diag(blocks):
    g, w, _ = blocks.shape
    out = jnp.zeros((g * w, g * w), blocks.dtype)
    for i in range(g):
        out = out.at[i * w:(i + 1) * w, i * w:(i + 1) * w].set(blocks[i])
    return out


def kernel(x, c, ctx, c_ctx, ada_w, ada_b, ffn1_w_gu, ffn1_w_down, w_in, q_gain, k_gain, pool_w, pool_scale,
           fft_w, w_out, ffn2_w_gu, ffn2_w_down, final_gain):
    bsz, seq, d = x.shape
    clen = ctx.shape[1]
    depth = ada_w.shape[0]

    rows = -(-(bsz + 1) // 8) * 8
    cc = jnp.zeros((rows, d), F32).at[:bsz].set(c).at[bsz].set(c_ctx)
    mod = _ada(cc, ada_w, ada_b).reshape(depth, N_MOD, rows, 1, d)

    w1gu, w1d = ffn1_w_gu.astype(BF16), ffn1_w_down.astype(BF16)
    w2gu, w2d = ffn2_w_gu.astype(BF16), ffn2_w_down.astype(BF16)
    wqkvT = jnp.swapaxes(w_in[:, :, :QKV_W], 1, 2).astype(BF16)
    wpf = w_in[:, :, QKV_W:].astype(BF16)
    qg = q_gain.reshape(depth, HEAD_DIM, 1)
    kg = k_gain.reshape(depth, HEAD_DIM, 1)
    poolbd = jax.vmap(_block_diag)(pool_w).astype(BF16)
    pscale = pool_scale.reshape(depth, 1, POOL_W)
    fftw = fft_w.astype(BF16)
    wout = w_out.astype(BF16)
    fg = final_gain.reshape(1, d)

    cosT, sinT = _rope_tables(seq)
    ones = jnp.ones((HALF, clen), F32)
    cc_, sc_ = _dft(FFT_GROUP_W)
    dftc = jnp.concatenate([_block_diag(jnp.stack([cc_] * N_FFT_GROUPS)),
                            _block_diag(jnp.stack([sc_] * N_FFT_GROUPS))], axis=1).astype(BF16)
    cl_x, sl_x = _dft(seq)
    cl_c, sl_c = _dft(clen)
    cl_x, nsl_x = cl_x.astype(BF16), (-sl_x).astype(BF16)
    cl_c, nsl_c = cl_c.astype(BF16), (-sl_c).astype(BF16)

    def ffn_x(l_arr, xv, ks, wgu, wd, final=False):
        return _ffn(l_arr, xv, mod, ks, True, bsz, wgu, wd, fg, final)

    def ffn_c(l_arr, cv, ks, wgu, wd):
        flat = cv.reshape(1, bsz * clen, d)
        return _ffn(l_arr, flat, mod, ks, False, bsz, wgu, wd, fg, False).reshape(bsz, clen, d)

    def layer(l_arr, xv, cv, last):
        xv = ffn_x(l_arr, xv, (0, 1, 2), w1gu, w1d)
        cv = ffn_c(l_arr, cv, (0, 1, 2), w1gu, w1d)
        qT, k, vT, up, xcs = _inproj(l_arr, xv, mod, True, bsz, wqkvT, wpf, qg, kg, cosT, sinT, dftc, True)
        qcT, kc, vcT, upc, xcsc = _inproj(l_arr, cv, mod, False, bsz, wqkvT, wpf, qg, kg, ones, ones, dftc, False)
        oT = _attn(qT, jnp.concatenate([k, kc], axis=1), jnp.concatenate([vT, vcT], axis=2))
        xv = _mix(l_arr, xv, mod, True, bsz, oT, up, xcs, cl_x, nsl_x, fftw, poolbd, pscale, wout)
        xv = ffn_x(l_arr, xv, (6, 7, 8), w2gu, w2d, final=last)
        if not last:
            ocT = _attn(qcT, kc, vcT)
            cv = _mix(l_arr, cv, mod, False, bsz, ocT, upc, xcsc, cl_c, nsl_c, fftw, poolbd, pscale, wout)
            cv = ffn_c(l_arr, cv, (6, 7, 8), w2gu, w2d)
        return xv, cv

    def body(carry, l):
        xv, cv = layer(l.reshape(1), carry[0], carry[1], False)
        return (xv, cv), None

    (x, ctx), _ = lax.scan(body, (x, ctx), jnp.arange(depth - 1, dtype=jnp.int32))
    x, _ = layer(jnp.full((1,), depth - 1, jnp.int32), x, ctx, True)
    return x
```

```python
import functools
import math

import jax
import jax.numpy as jnp
from jax import lax
from jax.experimental import pallas as pl
from jax.experimental.pallas import tpu as pltpu

F32 = jnp.float32
BF16 = jnp.bfloat16

GRID_W = 64
HEAD_DIM = 64
HALF = HEAD_DIM // 2
N_Q_HEADS = 8
N_KV_HEADS = 2
Q_PER_KV = N_Q_HEADS // N_KV_HEADS
ATTN_W = N_Q_HEADS * HEAD_DIM
KV_W = N_KV_HEADS * HEAD_DIM
QKV_W = ATTN_W + 2 * KV_W
POOL_W = 256
POOL_HALF_WINDOWS = (1, 2, 4, 8)
POOL_GROUP_W = POOL_W // len(POOL_HALF_WINDOWS)
POOL_PAD = 8
LANES = 128
FFT_W = 256
N_FFT_GROUPS = 4
FFT_GROUP_W = FFT_W // N_FFT_GROUPS
N_MOD = 9
ROPE_THETA = 10000.0
EPS = 1e-6
Q_SCALE = HEAD_DIM ** -0.5 * math.log2(math.e)

V7X_VMEM_LIMIT = 56 * 1024 * 1024
ROW_TILE = 512
Q_TILE = 512
KEY_CHUNK = 256
QK_LOOKAHEAD = 3
FFN_SUBTILES = 2
PROJ_SUBTILES = 1


def _dot(a, b):
    return jnp.dot(a, b, preferred_element_type=F32)


def _dot_nt(a, b):
    return lax.dot_general(a, b, (((1,), (1,)), ((), ())), preferred_element_type=F32)


def _dot_tn(a, b):
    return lax.dot_general(a, b, (((0,), (0,)), ((), ())), preferred_element_type=F32)


def _rms(x):
    return x * lax.rsqrt(jnp.mean(x * x, axis=-1, keepdims=True) + EPS)


def _resident(shape, index_map):
    return pl.BlockSpec(shape, index_map, pipeline_mode=pl.Buffered(1))


def _params(n_axes):
    return pltpu.CompilerParams(dimension_semantics=("arbitrary",) * n_axes,
                                vmem_limit_bytes=V7X_VMEM_LIMIT)


def _ada_kernel(cc_ref, w_ref, b_ref, o_ref):
    a = cc_ref[...]
    a = a / (1.0 + jnp.exp(-a))
    a_hi = a.astype(BF16)
    a_lo = (a - a_hi.astype(F32)).astype(BF16)
    w = w_ref[0]
    w_hi = w.astype(BF16)
    w_lo = (w - w_hi.astype(F32)).astype(BF16)
    acc = _dot(a_hi, w_hi) + _dot(a_lo, w_hi) + _dot(a_hi, w_lo)
    o_ref[0, 0] = acc + b_ref[0]


def _ada(cc, ada_w, ada_b):
    depth, d, _ = ada_w.shape
    rows = cc.shape[0]
    return pl.pallas_call(
        _ada_kernel,
        out_shape=jax.ShapeDtypeStruct((depth, N_MOD, rows, d), F32),
        grid=(depth, N_MOD),
        in_specs=[pl.BlockSpec((rows, d), lambda l, n: (0, 0)),
                  pl.BlockSpec((1, d, d), lambda l, n: (l, 0, n)),
                  pl.BlockSpec((1, 1, d), lambda l, n: (l, 0, n))],
        out_specs=pl.BlockSpec((1, 1, rows, d), lambda l, n: (l, n, 0, 0)),
        compiler_params=_params(2),
        name="ada_mod",
    )(cc, ada_w, ada_b.reshape(depth, 1, N_MOD * d))


def _mod_spec(k, per_batch, ctx_row):
    if per_batch:
        return lambda d: pl.BlockSpec((1, 1, 1, 1, d), lambda b, i, l: (l[0], k, b, 0, 0))
    return lambda d: pl.BlockSpec((1, 1, 1, 1, d), lambda b, i, l: (l[0], k, ctx_row, 0, 0))


def _ffn_kernel(l_ref, x_ref, shift_ref, scale_ref, gate_ref, wgu_ref, wd_ref, fg_ref, o_ref, *, final):
    sub = x_ref.shape[1] // FFN_SUBTILES
    f = wd_ref.shape[1]
    xs = [x_ref[0, s * sub:(s + 1) * sub, :] for s in range(FFN_SUBTILES)]
    hs = [(_rms(x) * (1.0 + scale_ref[0, 0, 0]) + shift_ref[0, 0, 0]).astype(BF16) for x in xs]
    gus = [_dot(h, wgu_ref[0]) for h in hs]
    acts = [(gu[:, :f] / (1.0 + jnp.exp(-gu[:, :f])) * gu[:, f:]).astype(BF16) for gu in gus]
    ys = [_dot(a, wd_ref[0]) for a in acts]
    for s, (x, y) in enumerate(zip(xs, ys)):
        out = x + (0.5 * gate_ref[0, 0, 0]) * y
        if final:
            out = _rms(out) * fg_ref[...]
        o_ref[0, s * sub:(s + 1) * sub, :] = out


def _ffn(l_arr, x, mod, mod_k, per_batch, ctx_row, wgu, wd, final_gain, final):
    bx, rows, d = x.shape
    tm = min(ROW_TILE, rows)
    f2 = wgu.shape[2]
    specs = [_mod_spec(k, per_batch, ctx_row)(d) for k in mod_k]
    return pl.pallas_call(
        functools.partial(_ffn_kernel, final=final),
        out_shape=jax.ShapeDtypeStruct(x.shape, F32),
        grid_spec=pltpu.PrefetchScalarGridSpec(
            num_scalar_prefetch=1,
            grid=(bx, rows // tm),
            in_specs=[pl.BlockSpec((1, tm, d), lambda b, i, l: (b, i, 0))] + specs + [
                _resident((1, d, f2), lambda b, i, l: (l[0], 0, 0)),
                _resident((1, f2 // 2, d), lambda b, i, l: (l[0], 0, 0)),
                pl.BlockSpec((1, d), lambda b, i, l: (0, 0))],
            out_specs=pl.BlockSpec((1, tm, d), lambda b, i, l: (b, i, 0))),
        compiler_params=_params(2),
        name="ffn_final" if final else "ffn",
    )(l_arr, x, mod, mod, mod, wgu, wd, final_gain)


def _inproj_kernel(l_ref, x_ref, shift_ref, scale_ref, wqkv_ref, wpf_ref, qg_ref, kg_ref, cos_ref, sin_ref,
                   dftc_ref, qT_ref, k_ref, vT_ref, up_ref, xcs_ref, *, rope):
    sub = x_ref.shape[1] // PROJ_SUBTILES
    spans = [slice(s * sub, (s + 1) * sub) for s in range(PROJ_SUBTILES)]
    hs = [(_rms(x_ref[0, sp, :]) * (1.0 + scale_ref[0, 0, 0]) + shift_ref[0, 0, 0]).astype(BF16) for sp in spans]
    zTs = [_dot_nt(wqkv_ref[0], h) for h in hs]
    z2s = [_dot(h, wpf_ref[0]) for h in hs]

    def head(zh, gain, sp):
        zn = zh * lax.rsqrt(jnp.mean(zh * zh, axis=0, keepdims=True) + EPS) * gain
        x1, x2 = zn[:HALF], zn[HALF:]
        if not rope:
            return x1, x2
        c, s = cos_ref[:, sp], sin_ref[:, sp]
        return x1 * c - x2 * s, x1 * s + x2 * c

    for sp, zT, z2 in zip(spans, zTs, z2s):
        for j in range(N_Q_HEADS):
            r = j * HEAD_DIM
            a, b = head(zT[r:r + HEAD_DIM], qg_ref[0], sp)
            qT_ref[0, r:r + HALF, sp] = (a * Q_SCALE).astype(BF16)
            qT_ref[0, r + HALF:r + HEAD_DIM, sp] = (b * Q_SCALE).astype(BF16)
        parts = []
        for j in range(N_KV_HEADS):
            r = ATTN_W + j * HEAD_DIM
            parts.extend(head(zT[r:r + HEAD_DIM], kg_ref[0], sp))
        k_ref[0, sp, :] = jnp.concatenate(parts, axis=0).T.astype(BF16)
        vT_ref[0, :, sp] = zT[ATTN_W + KV_W:].astype(BF16)
        up_ref[0, sp, :] = z2[:, :POOL_W]
        xcs_ref[0, sp, :] = _dot(z2[:, POOL_W:].astype(BF16), dftc_ref[...]).astype(BF16)


def _inproj(l_arr, x, mod, per_batch, ctx_row, wqkvT, wpf, qg, kg, cosT, sinT, dftc, rope):
    b_, rows, d = x.shape
    tm = min(ROW_TILE, rows)
    pf = POOL_W + FFT_W
    specs = [_mod_spec(k, per_batch, ctx_row)(d) for k in (3, 4)]
    out_shape = (jax.ShapeDtypeStruct((b_, ATTN_W, rows), BF16),
                 jax.ShapeDtypeStruct((b_, rows, KV_W), BF16),
                 jax.ShapeDtypeStruct((b_, KV_W, rows), BF16),
                 jax.ShapeDtypeStruct((b_, rows, POOL_W), F32),
                 jax.ShapeDtypeStruct((b_, rows, 2 * FFT_W), BF16))
    return pl.pallas_call(
        functools.partial(_inproj_kernel, rope=rope),
        out_shape=out_shape,
        grid_spec=pltpu.PrefetchScalarGridSpec(
            num_scalar_prefetch=1,
            grid=(b_, rows // tm),
            in_specs=[pl.BlockSpec((1, tm, d), lambda b, i, l: (b, i, 0))] + specs + [
                pl.BlockSpec((1, QKV_W, d), lambda b, i, l: (l[0], 0, 0)),
                pl.BlockSpec((1, d, pf), lambda b, i, l: (l[0], 0, 0)),
                pl.BlockSpec((1, HEAD_DIM, 1), lambda b, i, l: (l[0], 0, 0)),
                pl.BlockSpec((1, HEAD_DIM, 1), lambda b, i, l: (l[0], 0, 0)),
                pl.BlockSpec((HALF, tm), lambda b, i, l: (0, i)),
                pl.BlockSpec((HALF, tm), lambda b, i, l: (0, i)),
                pl.BlockSpec((FFT_W, 2 * FFT_W), lambda b, i, l: (0, 0))],
            out_specs=[pl.BlockSpec((1, ATTN_W, tm), lambda b, i, l: (b, 0, i)),
                       pl.BlockSpec((1, tm, KV_W), lambda b, i, l: (b, i, 0)),
                       pl.BlockSpec((1, KV_W, tm), lambda b, i, l: (b, 0, i)),
                       pl.BlockSpec((1, tm, POOL_W), lambda b, i, l: (b, i, 0)),
                       pl.BlockSpec((1, tm, 2 * FFT_W), lambda b, i, l: (b, i, 0))]),
        compiler_params=_params(2),
        name="in_proj_rope" if rope else "in_proj",
    )(l_arr, x, mod, mod, wqkvT, wpf, qg, kg, cosT, sinT, dftc)


def _attn_kernel(qT_ref, k_ref, vT_ref, o_ref, *, key_chunk):
    n_chunks = k_ref.shape[1] // key_chunk
    steps = [(j, c) for j in range(N_Q_HEADS) for c in range(n_chunks)]

    def scores(j, c):
        qj = qT_ref[0, j * HEAD_DIM:(j + 1) * HEAD_DIM, :]
        zero = jnp.zeros_like(qj)
        qb = jnp.concatenate([qj, zero] if j // Q_PER_KV == 0 else [zero, qj], axis=0)
        return _dot(k_ref[0, c * key_chunk:(c + 1) * key_chunk, :], qb)

    pending = [scores(*st) for st in steps[:QK_LOOKAHEAD]]
    for n, (j, c) in enumerate(steps):
        lo = (j // Q_PER_KV) * HEAD_DIM
        s = pending.pop(0)
        if n + QK_LOOKAHEAD < len(steps):
            pending.append(scores(*steps[n + QK_LOOKAHEAD]))
        cm = jnp.max(s, axis=0, keepdims=True)
        m_new = cm if c == 0 else jnp.maximum(m, cm)
        p = jnp.exp2(s - m_new)
        ps = jnp.sum(p, axis=0, keepdims=True)
        pv = _dot(vT_ref[0, :, c * key_chunk:(c + 1) * key_chunk], p.astype(BF16))[lo:lo + HEAD_DIM]
        if c == 0:
            den, acc = ps, pv
        else:
            alpha = jnp.exp2(m - m_new)
            den, acc = alpha * den + ps, alpha * acc + pv
        m = m_new
        if c == n_chunks - 1:
            o_ref[0, j * HEAD_DIM:(j + 1) * HEAD_DIM, :] = (acc / den).astype(BF16)


def _attn(qT, k, vT):
    b_, _, rows = qT.shape
    keys = k.shape[1]
    tq = min(Q_TILE, rows)
    return pl.pallas_call(
        functools.partial(_attn_kernel, key_chunk=min(KEY_CHUNK, keys)),
        out_shape=jax.ShapeDtypeStruct(qT.shape, BF16),
        grid=(b_, rows // tq),
        in_specs=[pl.BlockSpec((1, ATTN_W, tq), lambda b, i: (b, 0, i)),
                  pl.BlockSpec((1, keys, KV_W), lambda b, i: (b, 0, 0)),
                  pl.BlockSpec((1, KV_W, keys), lambda b, i: (b, 0, 0))],
        out_specs=pl.BlockSpec((1, ATTN_W, tq), lambda b, i: (b, 0, i)),
        compiler_params=_params(2),
        name="attn",
    )(qT, k, vT)


def _mix_kernel(l_ref, x_ref, gate_ref, oT_ref, up_ref, xcs_ref, cl_ref, nsl_ref, fftw_ref, poolw_ref, pscale_ref,
                wout_ref, o_ref, pad_ref, pm_ref, *, seq, tm):
    i = pl.program_id(1)

    @pl.when(i == 0)
    def _():
        n = seq + 2 * POOL_PAD
        zeros = jnp.zeros((POOL_PAD, POOL_W), F32)
        pad_ref[0:POOL_PAD, :] = zeros
        pad_ref[POOL_PAD + seq:n, :] = zeros
        pad_ref[POOL_PAD:POOL_PAD + seq, :] = up_ref[0]
        first_half = lax.broadcasted_iota(jnp.int32, (1, LANES), 1) < POOL_GROUP_W
        for tile in range(POOL_W // LANES):
            xe = pad_ref[:, tile * LANES:(tile + 1) * LANES]
            hws = POOL_HALF_WINDOWS[2 * tile:2 * tile + 2]
            fwd, w, centred = xe, 1, []
            while w < 2 * hws[-1]:
                fwd = fwd + pltpu.roll(fwd, n - w, axis=0)
                w *= 2
                if w // 2 in hws:
                    centred.append(pltpu.roll(fwd, w // 2, axis=0))
            win = jnp.where(first_half, centred[0], centred[1])
            hw = jnp.where(first_half, hws[0], hws[1])
            inner = win * (0.5 / hw.astype(F32)) - xe

            def edge(lo):
                t = lax.broadcasted_iota(jnp.int32, (POOL_PAD, LANES), 0) + (lo - POOL_PAD)
                cnt = (jnp.minimum(t + hw, seq) - jnp.maximum(t - hw, 0)).astype(F32)
                return win[lo:lo + POOL_PAD] / cnt - xe[lo:lo + POOL_PAD]

            out = jnp.concatenate([edge(POOL_PAD), inner[2 * POOL_PAD:seq], edge(seq)], axis=0)
            pm_ref[:, tile * LANES:(tile + 1) * LANES] = out.astype(BF16)

    sub = tm // PROJ_SUBTILES
    for s in range(PROJ_SUBTILES):
        sp = slice(s * sub, (s + 1) * sub)
        r0 = pl.multiple_of(i * tm + s * sub, sub)
        f = (_dot(cl_ref[pl.ds(r0, sub), :], xcs_ref[0, :, 0:FFT_W])
             + _dot(nsl_ref[pl.ds(r0, sub), :], xcs_ref[0, :, FFT_W:2 * FFT_W]))
        yf = _dot(f.astype(BF16), fftw_ref[0])
        yp = _dot(pm_ref[pl.ds(r0, sub), :], poolw_ref[0]) * pscale_ref[0]
        y = (_dot_tn(oT_ref[0, :, sp], wout_ref[0, 0:ATTN_W, :])
             + _dot(yp.astype(BF16), wout_ref[0, ATTN_W:ATTN_W + POOL_W, :])
             + _dot(yf.astype(BF16), wout_ref[0, ATTN_W + POOL_W:, :]))
        o_ref[0, sp, :] = x_ref[0, sp, :] + gate_ref[0, 0, 0] * y


def _mix(l_arr, x, mod, per_batch, ctx_row, oT, up, xcs, cl, nsl, fftw, poolbd, pscale, wout):
    b_, seq, d = x.shape
    tm = min(ROW_TILE, seq)
    return pl.pallas_call(
        functools.partial(_mix_kernel, seq=seq, tm=tm),
        out_shape=jax.ShapeDtypeStruct(x.shape, F32),
        grid_spec=pltpu.PrefetchScalarGridSpec(
            num_scalar_prefetch=1,
            grid=(b_, seq // tm),
            in_specs=[pl.BlockSpec((1, tm, d), lambda b, i, l: (b, i, 0)),
                      _mod_spec(5, per_batch, ctx_row)(d),
                      pl.BlockSpec((1, ATTN_W, tm), lambda b, i, l: (b, 0, i)),
                      pl.BlockSpec((1, seq, POOL_W), lambda b, i, l: (b, 0, 0)),
                      pl.BlockSpec((1, seq, 2 * FFT_W), lambda b, i, l: (b, 0, 0)),
                      _resident((seq, seq), lambda b, i, l: (0, 0)),
                      _resident((seq, seq), lambda b, i, l: (0, 0)),
                      pl.BlockSpec((1, FFT_W, FFT_W), lambda b, i, l: (l[0], 0, 0)),
                      pl.BlockSpec((1, POOL_W, POOL_W), lambda b, i, l: (l[0], 0, 0)),
                      pl.BlockSpec((1, 1, POOL_W), lambda b, i, l: (l[0], 0, 0)),
                      pl.BlockSpec((1, d, d), lambda b, i, l: (l[0], 0, 0))],
            out_specs=pl.BlockSpec((1, tm, d), lambda b, i, l: (b, i, 0)),
            scratch_shapes=[pltpu.VMEM((seq + 2 * POOL_PAD, POOL_W), F32),
                            pltpu.VMEM((seq, POOL_W), BF16)]),
        compiler_params=_params(2),
        name="mix",
    )(l_arr, x, mod, oT, up, xcs, cl, nsl, fftw, poolbd, pscale, wout)


def _rope_tables(length):
    t = jnp.arange(length)
    n_freq = HEAD_DIM // 4
    inv_freq = ROPE_THETA ** (-jnp.arange(n_freq, dtype=F32) / n_freq)
    row = (t // GRID_W).astype(F32)
    col = (t % GRID_W).astype(F32)
    ang = jnp.concatenate([inv_freq[:, None] * row[None, :], inv_freq[:, None] * col[None, :]], axis=0)
    return jnp.cos(ang), jnp.sin(ang)


def _dft(n):
    k = jnp.arange(n)
    ang = ((k[:, None] * k[None, :]) % n).astype(F32) * (2.0 * math.pi / n)
    return jnp.cos(ang) * n ** -0.5, jnp.sin(ang) * n ** -0.5


def _block_diag(blocks):
    g, w, _ = blocks.shape
    out = jnp.zeros((g * w, g * w), blocks.dtype)
    for i in range(g):
        out = out.at[i * w:(i + 1) * w, i * w:(i + 1) * w].set(blocks[i])
    return out


def kernel(x, c, ctx, c_ctx, ada_w, ada_b, ffn1_w_gu, ffn1_w_down, w_in, q_gain, k_gain, pool_w, pool_scale,
           fft_w, w_out, ffn2_w_gu, ffn2_w_down, final_gain):
    bsz, seq, d = x.shape
    clen = ctx.shape[1]
    depth = ada_w.shape[0]

    rows = -(-(bsz + 1) // 8) * 8
    cc = jnp.zeros((rows, d), F32).at[:bsz].set(c).at[bsz].set(c_ctx)
    mod = _ada(cc, ada_w, ada_b).reshape(depth, N_MOD, rows, 1, d)

    w1gu, w1d = ffn1_w_gu.astype(BF16), ffn1_w_down.astype(BF16)
    w2gu, w2d = ffn2_w_gu.astype(BF16), ffn2_w_down.astype(BF16)
    wqkvT = jnp.swapaxes(w_in[:, :, :QKV_W], 1, 2).astype(BF16)
    wpf = w_in[:, :, QKV_W:].astype(BF16)
    qg = q_gain.reshape(depth, HEAD_DIM, 1)
    kg = k_gain.reshape(depth, HEAD_DIM, 1)
    poolbd = jax.vmap(_block_diag)(pool_w).astype(BF16)
    pscale = pool_scale.reshape(depth, 1, POOL_W)
    fftw = fft_w.astype(BF16)
    wout = w_out.astype(BF16)
    fg = final_gain.reshape(1, d)

    cosT, sinT = _rope_tables(seq)
    ones = jnp.ones((HALF, clen), F32)
    cc_, sc_ = _dft(FFT_GROUP_W)
    dftc = jnp.concatenate([_block_diag(jnp.stack([cc_] * N_FFT_GROUPS)),
                            _block_diag(jnp.stack([sc_] * N_FFT_GROUPS))], axis=1).astype(BF16)
    cl_x, sl_x = _dft(seq)
    cl_c, sl_c = _dft(clen)
    cl_x, nsl_x = cl_x.astype(BF16), (-sl_x).astype(BF16)
    cl_c, nsl_c = cl_c.astype(BF16), (-sl_c).astype(BF16)
    cosT, sinT, dftc, cl_x, nsl_x, cl_c, nsl_c = lax.optimization_barrier(
        (cosT, sinT, dftc, cl_x, nsl_x, cl_c, nsl_c))

    def ffn_x(l_arr, xv, ks, wgu, wd, final=False):
        return _ffn(l_arr, xv, mod, ks, True, bsz, wgu, wd, fg, final)

    def ffn_c(l_arr, cv, ks, wgu, wd):
        flat = cv.reshape(1, bsz * clen, d)
        return _ffn(l_arr, flat, mod, ks, False, bsz, wgu, wd, fg, False).reshape(bsz, clen, d)

    def first_half_step(l_arr, xv, cv):
        return ffn_x(l_arr, xv, (0, 1, 2), w1gu, w1d), ffn_c(l_arr, cv, (0, 1, 2), w1gu, w1d)

    def rest_of_layer(l_arr, xv, cv, last):
        qT, k, vT, up, xcs = _inproj(l_arr, xv, mod, True, bsz, wqkvT, wpf, qg, kg, cosT, sinT, dftc, True)
        qcT, kc, vcT, upc, xcsc = _inproj(l_arr, cv, mod, False, bsz, wqkvT, wpf, qg, kg, ones, ones, dftc, False)
        oT = _attn(qT, jnp.concatenate([k, kc], axis=1), jnp.concatenate([vT, vcT], axis=2))
        xv = _mix(l_arr, xv, mod, True, bsz, oT, up, xcs, cl_x, nsl_x, fftw, poolbd, pscale, wout)
        xv = ffn_x(l_arr, xv, (6, 7, 8), w2gu, w2d, final=last)
        if not last:
            ocT = _attn(qcT, kc, vcT)
            cv = _mix(l_arr, cv, mod, False, bsz, ocT, upc, xcsc, cl_c, nsl_c, fftw, poolbd, pscale, wout)
            cv = ffn_c(l_arr, cv, (6, 7, 8), w2gu, w2d)
        return xv, cv

    def body(carry, l):
        xv, cv = rest_of_layer(l.reshape(1), carry[0], carry[1], False)
        return first_half_step(l.reshape(1) + 1, xv, cv), None

    carry = first_half_step(jnp.zeros((1,), jnp.int32), x, ctx)
    (x, ctx), _ = lax.scan(body, carry, jnp.arange(depth - 1, dtype=jnp.int32))
    x, _ = rest_of_layer(jnp.full((1,), depth - 1, jnp.int32), x, ctx, True)
    return x
```

```python
import functools
import math

import jax
import jax.numpy as jnp
from jax import lax
from jax.experimental import pallas as pl
from jax.experimental.pallas import tpu as pltpu

F32 = jnp.float32
BF16 = jnp.bfloat16

GRID_W = 64
HEAD_DIM = 64
HALF = HEAD_DIM // 2
N_Q_HEADS = 8
N_KV_HEADS = 2
Q_PER_KV = N_Q_HEADS // N_KV_HEADS
ATTN_W = N_Q_HEADS * HEAD_DIM
KV_W = N_KV_HEADS * HEAD_DIM
QKV_W = ATTN_W + 2 * KV_W
POOL_W = 256
POOL_HALF_WINDOWS = (1, 2, 4, 8)
POOL_GROUP_W = POOL_W // len(POOL_HALF_WINDOWS)
POOL_PAD = 8
LANES = 128
FFT_W = 256
N_FFT_GROUPS = 4
FFT_GROUP_W = FFT_W // N_FFT_GROUPS
N_MOD = 9
ROPE_THETA = 10000.0
EPS = 1e-6
Q_SCALE = HEAD_DIM ** -0.5 * math.log2(math.e)

V7X_VMEM_LIMIT = 56 * 1024 * 1024
ROW_TILE = 512
Q_TILE = 512
KEY_CHUNK = 256
QK_LOOKAHEAD = 2
BF16_SUBLANES = 16
FFN_SUBTILES = 2
PROJ_SUBTILES = 1


def _dot(a, b):
    return jnp.dot(a, b, preferred_element_type=F32)


def _dot_nt(a, b):
    return lax.dot_general(a, b, (((1,), (1,)), ((), ())), preferred_element_type=F32)


def _dot_tn(a, b):
    return lax.dot_general(a, b, (((0,), (0,)), ((), ())), preferred_element_type=F32)


def _rms(x):
    return x * lax.rsqrt(jnp.mean(x * x, axis=-1, keepdims=True) + EPS)


def _resident(shape, index_map):
    return pl.BlockSpec(shape, index_map, pipeline_mode=pl.Buffered(1))


def _params(n_axes):
    return pltpu.CompilerParams(dimension_semantics=("arbitrary",) * n_axes,
                                vmem_limit_bytes=V7X_VMEM_LIMIT)


def _ada_kernel(cc_ref, w_ref, b_ref, o_ref):
    a = cc_ref[...]
    a = a / (1.0 + jnp.exp(-a))
    a_hi = a.astype(BF16)
    a_lo = (a - a_hi.astype(F32)).astype(BF16)
    w = w_ref[0]
    w_hi = w.astype(BF16)
    w_lo = (w - w_hi.astype(F32)).astype(BF16)
    acc = _dot(a_hi, w_hi) + _dot(a_lo, w_hi) + _dot(a_hi, w_lo)
    o_ref[0, 0] = acc + b_ref[0]


def _ada(cc, ada_w, ada_b):
    depth, d, _ = ada_w.shape
    rows = cc.shape[0]
    return pl.pallas_call(
        _ada_kernel,
        out_shape=jax.ShapeDtypeStruct((depth, N_MOD, rows, d), F32),
        grid=(depth, N_MOD),
        in_specs=[pl.BlockSpec((rows, d), lambda l, n: (0, 0)),
                  pl.BlockSpec((1, d, d), lambda l, n: (l, 0, n)),
                  pl.BlockSpec((1, 1, d), lambda l, n: (l, 0, n))],
        out_specs=pl.BlockSpec((1, 1, rows, d), lambda l, n: (l, n, 0, 0)),
        compiler_params=_params(2),
        name="ada_mod",
    )(cc, ada_w, ada_b.reshape(depth, 1, N_MOD * d))


def _mod_spec(k, per_batch, ctx_row):
    if per_batch:
        return lambda d: pl.BlockSpec((1, 1, 1, 1, d), lambda b, i, l: (l[0], k, b, 0, 0))
    return lambda d: pl.BlockSpec((1, 1, 1, 1, d), lambda b, i, l: (l[0], k, ctx_row, 0, 0))


def _ffn_kernel(l_ref, x_ref, shift_ref, scale_ref, gate_ref, wgu_ref, wd_ref, fg_ref, o_ref, *, final):
    sub = x_ref.shape[1] // FFN_SUBTILES
    f = wd_ref.shape[1]
    xs = [x_ref[0, s * sub:(s + 1) * sub, :] for s in range(FFN_SUBTILES)]
    hs = [(_rms(x) * (1.0 + scale_ref[0, 0, 0]) + shift_ref[0, 0, 0]).astype(BF16) for x in xs]
    gus = [_dot(h, wgu_ref[0]) for h in hs]
    acts = [(gu[:, :f] / (1.0 + jnp.exp(-gu[:, :f])) * gu[:, f:]).astype(BF16) for gu in gus]
    ys = [_dot(a, wd_ref[0]) for a in acts]
    for s, (x, y) in enumerate(zip(xs, ys)):
        out = x + (0.5 * gate_ref[0, 0, 0]) * y
        if final:
            out = _rms(out) * fg_ref[...]
        o_ref[0, s * sub:(s + 1) * sub, :] = out


def _ffn(l_arr, x, mod, mod_k, per_batch, ctx_row, wgu, wd, final_gain, final):
    bx, rows, d = x.shape
    tm = min(ROW_TILE, rows)
    f2 = wgu.shape[2]
    specs = [_mod_spec(k, per_batch, ctx_row)(d) for k in mod_k]
    return pl.pallas_call(
        functools.partial(_ffn_kernel, final=final),
        out_shape=jax.ShapeDtypeStruct(x.shape, F32),
        grid_spec=pltpu.PrefetchScalarGridSpec(
            num_scalar_prefetch=1,
            grid=(bx, rows // tm),
            in_specs=[pl.BlockSpec((1, tm, d), lambda b, i, l: (b, i, 0))] + specs + [
                _resident((1, d, f2), lambda b, i, l: (l[0], 0, 0)),
                _resident((1, f2 // 2, d), lambda b, i, l: (l[0], 0, 0)),
                pl.BlockSpec((1, d), lambda b, i, l: (0, 0))],
            out_specs=pl.BlockSpec((1, tm, d), lambda b, i, l: (b, i, 0))),
        compiler_params=_params(2),
        name="ffn_final" if final else "ffn",
    )(l_arr, x, mod, mod, mod, wgu, wd, final_gain)


def _inproj_kernel(l_ref, x_ref, shift_ref, scale_ref, wqkv_ref, wpf_ref, qg_ref, kg_ref, cos_ref, sin_ref,
                   dftc_ref, qT_ref, k_ref, vT_ref, up_ref, xcs_ref, *, rope):
    sub = x_ref.shape[1] // PROJ_SUBTILES
    spans = [slice(s * sub, (s + 1) * sub) for s in range(PROJ_SUBTILES)]
    hs = [(_rms(x_ref[0, sp, :]) * (1.0 + scale_ref[0, 0, 0]) + shift_ref[0, 0, 0]).astype(BF16) for sp in spans]
    zTs = [_dot_nt(wqkv_ref[0], h) for h in hs]
    z2s = [_dot(h, wpf_ref[0]) for h in hs]

    def head(zh, gain, sp):
        zn = zh * lax.rsqrt(jnp.mean(zh * zh, axis=0, keepdims=True) + EPS) * gain
        x1, x2 = zn[:HALF], zn[HALF:]
        if not rope:
            return x1, x2
        c, s = cos_ref[:, sp], sin_ref[:, sp]
        return x1 * c - x2 * s, x1 * s + x2 * c

    for sp, zT, z2 in zip(spans, zTs, z2s):
        for j in range(N_Q_HEADS):
            r = j * HEAD_DIM
            a, b = head(zT[r:r + HEAD_DIM], qg_ref[0], sp)
            qT_ref[0, r:r + HALF, sp] = (a * Q_SCALE).astype(BF16)
            qT_ref[0, r + HALF:r + HEAD_DIM, sp] = (b * Q_SCALE).astype(BF16)
        parts = []
        for j in range(N_KV_HEADS):
            r = ATTN_W + j * HEAD_DIM
            parts.extend(head(zT[r:r + HEAD_DIM], kg_ref[0], sp))
        k_ref[0, sp, :] = jnp.concatenate(parts, axis=0).T.astype(BF16)
        vT_ref[0, :, sp] = zT[ATTN_W + KV_W:].astype(BF16)
        up_ref[0, sp, :] = z2[:, :POOL_W]
        xcs_ref[0, sp, :] = _dot(z2[:, POOL_W:].astype(BF16), dftc_ref[...]).astype(BF16)


def _inproj(l_arr, x, mod, per_batch, ctx_row, wqkvT, wpf, qg, kg, cosT, sinT, dftc, rope):
    b_, rows, d = x.shape
    tm = min(ROW_TILE, rows)
    pf = POOL_W + FFT_W
    specs = [_mod_spec(k, per_batch, ctx_row)(d) for k in (3, 4)]
    out_shape = (jax.ShapeDtypeStruct((b_, ATTN_W, rows), BF16),
                 jax.ShapeDtypeStruct((b_, rows, KV_W), BF16),
                 jax.ShapeDtypeStruct((b_, KV_W, rows), BF16),
                 jax.ShapeDtypeStruct((b_, rows, POOL_W), F32),
                 jax.ShapeDtypeStruct((b_, rows, 2 * FFT_W), BF16))
    return pl.pallas_call(
        functools.partial(_inproj_kernel, rope=rope),
        out_shape=out_shape,
        grid_spec=pltpu.PrefetchScalarGridSpec(
            num_scalar_prefetch=1,
            grid=(b_, rows // tm),
            in_specs=[pl.BlockSpec((1, tm, d), lambda b, i, l: (b, i, 0))] + specs + [
                pl.BlockSpec((1, QKV_W, d), lambda b, i, l: (l[0], 0, 0)),
                pl.BlockSpec((1, d, pf), lambda b, i, l: (l[0], 0, 0)),
                pl.BlockSpec((1, HEAD_DIM, 1), lambda b, i, l: (l[0], 0, 0)),
                pl.BlockSpec((1, HEAD_DIM, 1), lambda b, i, l: (l[0], 0, 0)),
                pl.BlockSpec((HALF, tm), lambda b, i, l: (0, i)),
                pl.BlockSpec((HALF, tm), lambda b, i, l: (0, i)),
                pl.BlockSpec((FFT_W, 2 * FFT_W), lambda b, i, l: (0, 0))],
            out_specs=[pl.BlockSpec((1, ATTN_W, tm), lambda b, i, l: (b, 0, i)),
                       pl.BlockSpec((1, tm, KV_W), lambda b, i, l: (b, i, 0)),
                       pl.BlockSpec((1, KV_W, tm), lambda b, i, l: (b, 0, i)),
                       pl.BlockSpec((1, tm, POOL_W), lambda b, i, l: (b, i, 0)),
                       pl.BlockSpec((1, tm, 2 * FFT_W), lambda b, i, l: (b, i, 0))]),
        compiler_params=_params(2),
        name="in_proj_rope" if rope else "in_proj",
    )(l_arr, x, mod, mod, wqkvT, wpf, qg, kg, cosT, sinT, dftc)


def _attn_kernel(qT_ref, k_ref, vT_ref, o_ref, *, key_chunk):
    n_chunks = k_ref.shape[1] // key_chunk
    steps = [(j, c) for j in range(N_Q_HEADS) for c in range(n_chunks)]

    def scores(j, c):
        qj = qT_ref[0, j * HEAD_DIM:(j + 1) * HEAD_DIM, :]
        zero = jnp.zeros_like(qj)
        qb = jnp.concatenate([qj, zero] if j // Q_PER_KV == 0 else [zero, qj], axis=0)
        return _dot(k_ref[0, c * key_chunk:(c + 1) * key_chunk, :], qb)

    pending = [scores(*st) for st in steps[:QK_LOOKAHEAD]]
    for n, (j, c) in enumerate(steps):
        lo = (j // Q_PER_KV) * HEAD_DIM
        s = pending.pop(0)
        if n + QK_LOOKAHEAD < len(steps):
            pending.append(scores(*steps[n + QK_LOOKAHEAD]))
        cm = jnp.max(s, axis=0, keepdims=True)
        m_new = cm if c == 0 else jnp.maximum(m, cm)
        p = jnp.exp2(s - m_new)
        vaug = jnp.concatenate([vT_ref[0, lo:lo + HEAD_DIM, c * key_chunk:(c + 1) * key_chunk],
                                jnp.ones((BF16_SUBLANES, key_chunk), BF16)], axis=0)
        pvf = _dot(vaug, p.astype(BF16))
        pv, ps = pvf[:HEAD_DIM], pvf[HEAD_DIM:HEAD_DIM + 1]
        if c == 0:
            den, acc = ps, pv
        else:
            alpha = jnp.exp2(m - m_new)
            den, acc = alpha * den + ps, alpha * acc + pv
        m = m_new
        if c == n_chunks - 1:
            o_ref[0, j * HEAD_DIM:(j + 1) * HEAD_DIM, :] = (acc / den).astype(BF16)


def _attn(qT, k, vT):
    b_, _, rows = qT.shape
    keys = k.shape[1]
    tq = min(Q_TILE, rows)
    return pl.pallas_call(
        functools.partial(_attn_kernel, key_chunk=min(KEY_CHUNK, keys)),
        out_shape=jax.ShapeDtypeStruct(qT.shape, BF16),
        grid=(b_, rows // tq),
        in_specs=[pl.BlockSpec((1, ATTN_W, tq), lambda b, i: (b, 0, i)),
                  pl.BlockSpec((1, keys, KV_W), lambda b, i: (b, 0, 0)),
                  pl.BlockSpec((1, KV_W, keys), lambda b, i: (b, 0, 0))],
        out_specs=pl.BlockSpec((1, ATTN_W, tq), lambda b, i: (b, 0, i)),
        compiler_params=_params(2),
        name="attn",
    )(qT, k, vT)


def _mix_kernel(l_ref, x_ref, gate_ref, oT_ref, up_ref, xcs_ref, cl_ref, nsl_ref, fftw_ref, poolw_ref, pscale_ref,
                wout_ref, o_ref, pad_ref, pm_ref, *, seq, tm):
    i = pl.program_id(1)

    @pl.when(i == 0)
    def _():
        n = seq + 2 * POOL_PAD
        zeros = jnp.zeros((POOL_PAD, POOL_W), F32)
        pad_ref[0:POOL_PAD, :] = zeros
        pad_ref[POOL_PAD + seq:n, :] = zeros
        pad_ref[POOL_PAD:POOL_PAD + seq, :] = up_ref[0]
        first_half = lax.broadcasted_iota(jnp.int32, (1, LANES), 1) < POOL_GROUP_W
        for tile in range(POOL_W // LANES):
            xe = pad_ref[:, tile * LANES:(tile + 1) * LANES]
            hws = POOL_HALF_WINDOWS[2 * tile:2 * tile + 2]
            fwd, w, centred = xe, 1, []
            while w < 2 * hws[-1]:
                fwd = fwd + pltpu.roll(fwd, n - w, axis=0)
                w *= 2
                if w // 2 in hws:
                    centred.append(pltpu.roll(fwd, w // 2, axis=0))
            win = jnp.where(first_half, centred[0], centred[1])
            hw = jnp.where(first_half, hws[0], hws[1])
            inner = win * (0.5 / hw.astype(F32)) - xe

            def edge(lo):
                t = lax.broadcasted_iota(jnp.int32, (POOL_PAD, LANES), 0) + (lo - POOL_PAD)
                cnt = (jnp.minimum(t + hw, seq) - jnp.maximum(t - hw, 0)).astype(F32)
                return win[lo:lo + POOL_PAD] / cnt - xe[lo:lo + POOL_PAD]

            out = jnp.concatenate([edge(POOL_PAD), inner[2 * POOL_PAD:seq], edge(seq)], axis=0)
            pm_ref[:, tile * LANES:(tile + 1) * LANES] = out.astype(BF16)

    sub = tm // PROJ_SUBTILES
    for s in range(PROJ_SUBTILES):
        sp = slice(s * sub, (s + 1) * sub)
        r0 = pl.multiple_of(i * tm + s * sub, sub)
        f = (_dot(cl_ref[pl.ds(r0, sub), :], xcs_ref[0, :, 0:FFT_W])
             + _dot(nsl_ref[pl.ds(r0, sub), :], xcs_ref[0, :, FFT_W:2 * FFT_W]))
        yf = _dot(f.astype(BF16), fftw_ref[0])
        yp = _dot(pm_ref[pl.ds(r0, sub), :], poolw_ref[0]) * pscale_ref[0]
        y = (_dot_tn(oT_ref[0, :, sp], wout_ref[0, 0:ATTN_W, :])
             + _dot(yp.astype(BF16), wout_ref[0, ATTN_W:ATTN_W + POOL_W, :])
             + _dot(yf.astype(BF16), wout_ref[0, ATTN_W + POOL_W:, :]))
        o_ref[0, sp, :] = x_ref[0, sp, :] + gate_ref[0, 0, 0] * y


def _mix(l_arr, x, mod, per_batch, ctx_row, oT, up, xcs, cl, nsl, fftw, poolbd, pscale, wout):
    b_, seq, d = x.shape
    tm = min(ROW_TILE, seq)
    return pl.pallas_call(
        functools.partial(_mix_kernel, seq=seq, tm=tm),
        out_shape=jax.ShapeDtypeStruct(x.shape, F32),
        grid_spec=pltpu.PrefetchScalarGridSpec(
            num_scalar_prefetch=1,
            grid=(b_, seq // tm),
            in_specs=[pl.BlockSpec((1, tm, d), lambda b, i, l: (b, i, 0)),
                      _mod_spec(5, per_batch, ctx_row)(d),
                      pl.BlockSpec((1, ATTN_W, tm), lambda b, i, l: (b, 0, i)),
                      pl.BlockSpec((1, seq, POOL_W), lambda b, i, l: (b, 0, 0)),
                      pl.BlockSpec((1, seq, 2 * FFT_W), lambda b, i, l: (b, 0, 0)),
                      _resident((seq, seq), lambda b, i, l: (0, 0)),
                      _resident((seq, seq), lambda b, i, l: (0, 0)),
                      pl.BlockSpec((1, FFT_W, FFT_W), lambda b, i, l: (l[0], 0, 0)),
                      pl.BlockSpec((1, POOL_W, POOL_W), lambda b, i, l: (l[0], 0, 0)),
                      pl.BlockSpec((1, 1, POOL_W), lambda b, i, l: (l[0], 0, 0)),
                      pl.BlockSpec((1, d, d), lambda b, i, l: (l[0], 0, 0))],
            out_specs=pl.BlockSpec((1, tm, d), lambda b, i, l: (b, i, 0)),
            scratch_shapes=[pltpu.VMEM((seq + 2 * POOL_PAD, POOL_W), F32),
                            pltpu.VMEM((seq, POOL_W), BF16)]),
        compiler_params=_params(2),
        name="mix",
    )(l_arr, x, mod, oT, up, xcs, cl, nsl, fftw, poolbd, pscale, wout)


def _rope_tables(length):
    t = jnp.arange(length)
    n_freq = HEAD_DIM // 4
    inv_freq = ROPE_THETA ** (-jnp.arange(n_freq, dtype=F32) / n_freq)
    row = (t // GRID_W).astype(F32)
    col = (t % GRID_W).astype(F32)
    ang = jnp.concatenate([inv_freq[:, None] * row[None, :], inv_freq[:, None] * col[None, :]], axis=0)
    return jnp.cos(ang), jnp.sin(ang)


def _dft(n):
    k = jnp.arange(n)
    ang = ((k[:, None] * k[None, :]) % n).astype(F32) * (2.0 * math.pi / n)
    return jnp.cos(ang) * n ** -0.5, jnp.sin(ang) * n ** -0.5


def _block_diag(blocks):
    g, w, _ = blocks.shape
    out = jnp.zeros((g * w, g * w), blocks.dtype)
    for i in range(g):
        out = out.at[i * w:(i + 1) * w, i * w:(i + 1) * w].set(blocks[i])
    return out


def kernel(x, c, ctx, c_ctx, ada_w, ada_b, ffn1_w_gu, ffn1_w_down, w_in, q_gain, k_gain, pool_w, pool_scale,
           fft_w, w_out, ffn2_w_gu, ffn2_w_down, final_gain):
    bsz, seq, d = x.shape
    clen = ctx.shape[1]
    depth = ada_w.shape[0]

    rows = -(-(bsz + 1) // 8) * 8
    cc = jnp.zeros((rows, d), F32).at[:bsz].set(c).at[bsz].set(c_ctx)
    mod = _ada(cc, ada_w, ada_b).reshape(depth, N_MOD, rows, 1, d)

    w1gu, w1d = ffn1_w_gu.astype(BF16), ffn1_w_down.astype(BF16)
    w2gu, w2d = ffn2_w_gu.astype(BF16), ffn2_w_down.astype(BF16)
    wqkvT = jnp.swapaxes(w_in[:, :, :QKV_W], 1, 2).astype(BF16)
    wpf = w_in[:, :, QKV_W:].astype(BF16)
    qg = q_gain.reshape(depth, HEAD_DIM, 1)
    kg = k_gain.reshape(depth, HEAD_DIM, 1)
    poolbd = jax.vmap(_block_diag)(pool_w).astype(BF16)
    pscale = pool_scale.reshape(depth, 1, POOL_W)
    fftw = fft_w.astype(BF16)
    wout = w_out.astype(BF16)
    fg = final_gain.reshape(1, d)

    cosT, sinT = _rope_tables(seq)
    ones = jnp.ones((HALF, clen), F32)
    cc_, sc_ = _dft(FFT_GROUP_W)
    dftc = jnp.concatenate([_block_diag(jnp.stack([cc_] * N_FFT_GROUPS)),
                            _block_diag(jnp.stack([sc_] * N_FFT_GROUPS))], axis=1).astype(BF16)
    cl_x, sl_x = _dft(seq)
    cl_c, sl_c = _dft(clen)
    cl_x, nsl_x = cl_x.astype(BF16), (-sl_x).astype(BF16)
    cl_c, nsl_c = cl_c.astype(BF16), (-sl_c).astype(BF16)
    cosT, sinT, dftc, cl_x, nsl_x, cl_c, nsl_c = lax.optimization_barrier(
        (cosT, sinT, dftc, cl_x, nsl_x, cl_c, nsl_c))

    def ffn_x(l_arr, xv, ks, wgu, wd, final=False):
        return _ffn(l_arr, xv, mod, ks, True, bsz, wgu, wd, fg, final)

    def ffn_c(l_arr, cv, ks, wgu, wd):
        flat = cv.reshape(1, bsz * clen, d)
        return _ffn(l_arr, flat, mod, ks, False, bsz, wgu, wd, fg, False).reshape(bsz, clen, d)

    def first_half_step(l_arr, xv, cv):
        return ffn_x(l_arr, xv, (0, 1, 2), w1gu, w1d), ffn_c(l_arr, cv, (0, 1, 2), w1gu, w1d)

    def rest_of_layer(l_arr, xv, cv, last):
        qT, k, vT, up, xcs = _inproj(l_arr, xv, mod, True, bsz, wqkvT, wpf, qg, kg, cosT, sinT, dftc, True)
        qcT, kc, vcT, upc, xcsc = _inproj(l_arr, cv, mod, False, bsz, wqkvT, wpf, qg, kg, ones, ones, dftc, False)
        oT = _attn(qT, jnp.concatenate([k, kc], axis=1), jnp.concatenate([vT, vcT], axis=2))
        xv = _mix(l_arr, xv, mod, True, bsz, oT, up, xcs, cl_x, nsl_x, fftw, poolbd, pscale, wout)
        xv = ffn_x(l_arr, xv, (6, 7, 8), w2gu, w2d, final=last)
        if not last:
            ocT = _attn(qcT, kc, vcT)
            cv = _mix(l_arr, cv, mod, False, bsz, ocT, upc, xcsc, cl_c, nsl_c, fftw, poolbd, pscale, wout)
            cv = ffn_c(l_arr, cv, (6, 7, 8), w2gu, w2d)
        return xv, cv

    def body(carry, l):
        xv, cv = rest_of_layer(l.reshape(1), carry[0], carry[1], False)
        return first_half_step(l.reshape(1) + 1, xv, cv), None

    carry = first_half_step(jnp.zeros((1,), jnp.int32), x, ctx)
    (x, ctx), _ = lax.scan(body, carry, jnp.arange(depth - 1, dtype=jnp.int32))
    x, _ = rest_of_layer(jnp.full((1,), depth - 1, jnp.int32), x, ctx, True)
    return x
```

```python
import functools
import math

import jax
import jax.numpy as jnp
from jax import lax
from jax.experimental import pallas as pl
from jax.experimental.pallas import tpu as pltpu

F32 = jnp.float32
BF16 = jnp.bfloat16

GRID_W = 64
HEAD_DIM = 64
HALF = HEAD_DIM // 2
N_Q_HEADS = 8
N_KV_HEADS = 2
Q_PER_KV = N_Q_HEADS // N_KV_HEADS
ATTN_W = N_Q_HEADS * HEAD_DIM
KV_W = N_KV_HEADS * HEAD_DIM
QKV_W = ATTN_W + 2 * KV_W
POOL_W = 256
POOL_HALF_WINDOWS = (1, 2, 4, 8)
POOL_GROUP_W = POOL_W // len(POOL_HALF_WINDOWS)
POOL_PAD = 8
LANES = 128
FFT_W = 256
N_FFT_GROUPS = 4
FFT_GROUP_W = FFT_W // N_FFT_GROUPS
N_MOD = 9
ROPE_THETA = 10000.0
EPS = 1e-6
Q_SCALE = HEAD_DIM ** -0.5 * math.log2(math.e)

V7X_VMEM_LIMIT = 56 * 1024 * 1024
ROW_TILE = 1024
Q_TILE = 512
KEY_CHUNK = 256
QK_LOOKAHEAD = 2
BF16_SUBLANES = 16
FFN_SUB_ROWS = 256
PROJ_SUBTILES = 1


def _dot(a, b):
    return jnp.dot(a, b, preferred_element_type=F32)


def _dot_nt(a, b):
    return lax.dot_general(a, b, (((1,), (1,)), ((), ())), preferred_element_type=F32)


def _dot_tn(a, b):
    return lax.dot_general(a, b, (((0,), (0,)), ((), ())), preferred_element_type=F32)


def _rms(x):
    return x * lax.rsqrt(jnp.mean(x * x, axis=-1, keepdims=True) + EPS)


def _resident(shape, index_map):
    return pl.BlockSpec(shape, index_map, pipeline_mode=pl.Buffered(1))


def _params(n_axes):
    return pltpu.CompilerParams(dimension_semantics=("arbitrary",) * n_axes,
                                vmem_limit_bytes=V7X_VMEM_LIMIT)


def _ada_kernel(cc_ref, w_ref, b_ref, o_ref):
    a = cc_ref[...]
    a = a / (1.0 + jnp.exp(-a))
    a_hi = a.astype(BF16)
    a_lo = (a - a_hi.astype(F32)).astype(BF16)
    w = w_ref[0]
    w_hi = w.astype(BF16)
    w_lo = (w - w_hi.astype(F32)).astype(BF16)
    acc = _dot(a_hi, w_hi) + _dot(a_lo, w_hi) + _dot(a_hi, w_lo)
    o_ref[0, 0] = acc + b_ref[0]


def _ada(cc, ada_w, ada_b):
    depth, d, _ = ada_w.shape
    rows = cc.shape[0]
    return pl.pallas_call(
        _ada_kernel,
        out_shape=jax.ShapeDtypeStruct((depth, N_MOD, rows, d), F32),
        grid=(depth, N_MOD),
        in_specs=[pl.BlockSpec((rows, d), lambda l, n: (0, 0)),
                  pl.BlockSpec((1, d, d), lambda l, n: (l, 0, n)),
                  pl.BlockSpec((1, 1, d), lambda l, n: (l, 0, n))],
        out_specs=pl.BlockSpec((1, 1, rows, d), lambda l, n: (l, n, 0, 0)),
        compiler_params=_params(2),
        name="ada_mod",
    )(cc, ada_w, ada_b.reshape(depth, 1, N_MOD * d))


def _mod_spec(k, per_batch, ctx_row):
    if per_batch:
        return lambda d: pl.BlockSpec((1, 1, 1, 1, d), lambda b, i, l: (l[0], k, b, 0, 0))
    return lambda d: pl.BlockSpec((1, 1, 1, 1, d), lambda b, i, l: (l[0], k, ctx_row, 0, 0))


def _ffn_kernel(l_ref, x_ref, shift_ref, scale_ref, gate_ref, wgu_ref, wd_ref, fg_ref, o_ref, *, final):
    tm = x_ref.shape[1]
    sub = min(FFN_SUB_ROWS, tm)
    f = wd_ref.shape[1]
    xs = [x_ref[0, r:r + sub, :] for r in range(0, tm, sub)]
    hs = [(_rms(x) * (1.0 + scale_ref[0, 0, 0]) + shift_ref[0, 0, 0]).astype(BF16) for x in xs]
    gus = [_dot(h, wgu_ref[0]) for h in hs]
    acts = [(gu[:, :f] / (1.0 + jnp.exp(-gu[:, :f])) * gu[:, f:]).astype(BF16) for gu in gus]
    ys = [_dot(a, wd_ref[0]) for a in acts]
    for s, (x, y) in enumerate(zip(xs, ys)):
        out = x + (0.5 * gate_ref[0, 0, 0]) * y
        if final:
            out = _rms(out) * fg_ref[...]
        o_ref[0, s * sub:(s + 1) * sub, :] = out


def _ffn(l_arr, x, mod, mod_k, per_batch, ctx_row, wgu, wd, final_gain, final):
    bx, rows, d = x.shape
    tm = min(ROW_TILE, rows)
    f2 = wgu.shape[2]
    specs = [_mod_spec(k, per_batch, ctx_row)(d) for k in mod_k]
    return pl.pallas_call(
        functools.partial(_ffn_kernel, final=final),
        out_shape=jax.ShapeDtypeStruct(x.shape, F32),
        grid_spec=pltpu.PrefetchScalarGridSpec(
            num_scalar_prefetch=1,
            grid=(bx, rows // tm),
            in_specs=[pl.BlockSpec((1, tm, d), lambda b, i, l: (b, i, 0))] + specs + [
                _resident((1, d, f2), lambda b, i, l: (l[0], 0, 0)),
                _resident((1, f2 // 2, d), lambda b, i, l: (l[0], 0, 0)),
                pl.BlockSpec((1, d), lambda b, i, l: (0, 0))],
            out_specs=pl.BlockSpec((1, tm, d), lambda b, i, l: (b, i, 0))),
        compiler_params=_params(2),
        name="ffn_final" if final else "ffn",
    )(l_arr, x, mod, mod, mod, wgu, wd, final_gain)


def _inproj_kernel(l_ref, x_ref, shift_ref, scale_ref, wqkv_ref, wpf_ref, qg_ref, kg_ref, cos_ref, sin_ref,
                   dftc_ref, qT_ref, k_ref, vT_ref, up_ref, xcs_ref, *, rope):
    sub = x_ref.shape[1] // PROJ_SUBTILES
    spans = [slice(s * sub, (s + 1) * sub) for s in range(PROJ_SUBTILES)]
    hs = [(_rms(x_ref[0, sp, :]) * (1.0 + scale_ref[0, 0, 0]) + shift_ref[0, 0, 0]).astype(BF16) for sp in spans]
    zTs = [_dot_nt(wqkv_ref[0], h) for h in hs]
    z2s = [_dot(h, wpf_ref[0]) for h in hs]

    def head(zh, gain, sp):
        zn = zh * lax.rsqrt(jnp.mean(zh * zh, axis=0, keepdims=True) + EPS) * gain
        x1, x2 = zn[:HALF], zn[HALF:]
        if not rope:
            return x1, x2
        c, s = cos_ref[:, sp], sin_ref[:, sp]
        return x1 * c - x2 * s, x1 * s + x2 * c

    for sp, zT, z2 in zip(spans, zTs, z2s):
        for j in range(N_Q_HEADS):
            r = j * HEAD_DIM
            a, b = head(zT[r:r + HEAD_DIM], qg_ref[0], sp)
            qT_ref[0, r:r + HALF, sp] = (a * Q_SCALE).astype(BF16)
            qT_ref[0, r + HALF:r + HEAD_DIM, sp] = (b * Q_SCALE).astype(BF16)
        parts = []
        for j in range(N_KV_HEADS):
            r = ATTN_W + j * HEAD_DIM
            parts.extend(head(zT[r:r + HEAD_DIM], kg_ref[0], sp))
        k_ref[0, sp, :] = jnp.concatenate(parts, axis=0).T.astype(BF16)
        vT_ref[0, :, sp] = zT[ATTN_W + KV_W:].astype(BF16)
        up_ref[0, sp, :] = z2[:, :POOL_W]
        xcs_ref[0, sp, :] = _dot(z2[:, POOL_W:].astype(BF16), dftc_ref[...]).astype(BF16)


def _inproj(l_arr, x, mod, per_batch, ctx_row, wqkvT, wpf, qg, kg, cosT, sinT, dftc, rope):
    b_, rows, d = x.shape
    tm = min(ROW_TILE, rows)
    pf = POOL_W + FFT_W
    specs = [_mod_spec(k, per_batch, ctx_row)(d) for k in (3, 4)]
    out_shape = (jax.ShapeDtypeStruct((b_, ATTN_W, rows), BF16),
                 jax.ShapeDtypeStruct((b_, rows, KV_W), BF16),
                 jax.ShapeDtypeStruct((b_, KV_W, rows), BF16),
                 jax.ShapeDtypeStruct((b_, rows, POOL_W), F32),
                 jax.ShapeDtypeStruct((b_, rows, 2 * FFT_W), BF16))
    return pl.pallas_call(
        functools.partial(_inproj_kernel, rope=rope),
        out_shape=out_shape,
        grid_spec=pltpu.PrefetchScalarGridSpec(
            num_scalar_prefetch=1,
            grid=(b_, rows // tm),
            in_specs=[pl.BlockSpec((1, tm, d), lambda b, i, l: (b, i, 0))] + specs + [
                pl.BlockSpec((1, QKV_W, d), lambda b, i, l: (l[0], 0, 0)),
                pl.BlockSpec((1, d, pf), lambda b, i, l: (l[0], 0, 0)),
                pl.BlockSpec((1, HEAD_DIM, 1), lambda b, i, l: (l[0], 0, 0)),
                pl.BlockSpec((1, HEAD_DIM, 1), lambda b, i, l: (l[0], 0, 0)),
                pl.BlockSpec((HALF, tm), lambda b, i, l: (0, i)),
                pl.BlockSpec((HALF, tm), lambda b, i, l: (0, i)),
                pl.BlockSpec((FFT_W, 2 * FFT_W), lambda b, i, l: (0, 0))],
            out_specs=[pl.BlockSpec((1, ATTN_W, tm), lambda b, i, l: (b, 0, i)),
                       pl.BlockSpec((1, tm, KV_W), lambda b, i, l: (b, i, 0)),
                       pl.BlockSpec((1, KV_W, tm), lambda b, i, l: (b, 0, i)),
                       pl.BlockSpec((1, tm, POOL_W), lambda b, i, l: (b, i, 0)),
                       pl.BlockSpec((1, tm, 2 * FFT_W), lambda b, i, l: (b, i, 0))]),
        compiler_params=_params(2),
        name="in_proj_rope" if rope else "in_proj",
    )(l_arr, x, mod, mod, wqkvT, wpf, qg, kg, cosT, sinT, dftc)


def _attn_kernel(*refs, n_seg, key_chunk):
    qT_ref, k_refs, vT_refs, o_ref = refs[0], refs[1:1 + n_seg], refs[1 + n_seg:1 + 2 * n_seg], refs[-1]
    chunks = [(k_ref, vT_ref, r) for k_ref, vT_ref in zip(k_refs, vT_refs)
              for r in range(0, k_ref.shape[1], key_chunk)]
    steps = [(j, c) for j in range(N_Q_HEADS) for c in range(len(chunks))]

    def scores(j, c):
        k_ref, _, r = chunks[c]
        qj = qT_ref[0, j * HEAD_DIM:(j + 1) * HEAD_DIM, :]
        zero = jnp.zeros_like(qj)
        qb = jnp.concatenate([qj, zero] if j // Q_PER_KV == 0 else [zero, qj], axis=0)
        return _dot(k_ref[0, r:r + key_chunk, :], qb)

    pending = [scores(*st) for st in steps[:QK_LOOKAHEAD]]
    for n, (j, c) in enumerate(steps):
        _, vT_ref, r = chunks[c]
        lo = (j // Q_PER_KV) * HEAD_DIM
        s = pending.pop(0)
        if n + QK_LOOKAHEAD < len(steps):
            pending.append(scores(*steps[n + QK_LOOKAHEAD]))
        cm = jnp.max(s, axis=0, keepdims=True)
        m_new = cm if c == 0 else jnp.maximum(m, cm)
        p = jnp.exp2(s - m_new)
        vaug = jnp.concatenate([vT_ref[0, lo:lo + HEAD_DIM, r:r + key_chunk],
                                jnp.ones((BF16_SUBLANES, key_chunk), BF16)], axis=0)
        pvf = _dot(vaug, p.astype(BF16))
        pv, ps = pvf[:HEAD_DIM], pvf[HEAD_DIM:HEAD_DIM + 1]
        if c == 0:
            den, acc = ps, pv
        else:
            alpha = jnp.exp2(m - m_new)
            den, acc = alpha * den + ps, alpha * acc + pv
        m = m_new
        if c == len(chunks) - 1:
            o_ref[0, j * HEAD_DIM:(j + 1) * HEAD_DIM, :] = (acc / den).astype(BF16)


def _attn(qT, ks, vTs):
    b_, _, rows = qT.shape
    tq = min(Q_TILE, rows)
    key_chunk = min([KEY_CHUNK] + [k.shape[1] for k in ks])
    assert all(k.shape[1] % key_chunk == 0 for k in ks)
    return pl.pallas_call(
        functools.partial(_attn_kernel, n_seg=len(ks), key_chunk=key_chunk),
        out_shape=jax.ShapeDtypeStruct(qT.shape, BF16),
        grid=(b_, rows // tq),
        in_specs=([pl.BlockSpec((1, ATTN_W, tq), lambda b, i: (b, 0, i))]
                  + [pl.BlockSpec((1, k.shape[1], KV_W), lambda b, i: (b, 0, 0)) for k in ks]
                  + [pl.BlockSpec((1, KV_W, v.shape[2]), lambda b, i: (b, 0, 0)) for v in vTs]),
        out_specs=pl.BlockSpec((1, ATTN_W, tq), lambda b, i: (b, 0, i)),
        compiler_params=_params(2),
        name="attn",
    )(qT, *ks, *vTs)


def _mix_kernel(l_ref, x_ref, gate_ref, oT_ref, up_ref, xcs_ref, cl_ref, nsl_ref, fftw_ref, poolw_ref, pscale_ref,
                wout_ref, o_ref, pad_ref, pm_ref, *, seq, tm):
    i = pl.program_id(1)

    @pl.when(i == 0)
    def _():
        n = seq + 2 * POOL_PAD
        zeros = jnp.zeros((POOL_PAD, POOL_W), F32)
        pad_ref[0:POOL_PAD, :] = zeros
        pad_ref[POOL_PAD + seq:n, :] = zeros
        pad_ref[POOL_PAD:POOL_PAD + seq, :] = up_ref[0]
        first_half = lax.broadcasted_iota(jnp.int32, (1, LANES), 1) < POOL_GROUP_W
        for tile in range(POOL_W // LANES):
            xe = pad_ref[:, tile * LANES:(tile + 1) * LANES]
            hws = POOL_HALF_WINDOWS[2 * tile:2 * tile + 2]
            fwd, w, centred = xe, 1, []
            while w < 2 * hws[-1]:
                fwd = fwd + pltpu.roll(fwd, n - w, axis=0)
                w *= 2
                if w // 2 in hws:
                    centred.append(pltpu.roll(fwd, w // 2, axis=0))
            win = jnp.where(first_half, centred[0], centred[1])
            hw = jnp.where(first_half, hws[0], hws[1])
            inner = win * (0.5 / hw.astype(F32)) - xe

            def edge(lo):
                t = lax.broadcasted_iota(jnp.int32, (POOL_PAD, LANES), 0) + (lo - POOL_PAD)
                cnt = (jnp.minimum(t + hw, seq) - jnp.maximum(t - hw, 0)).astype(F32)
                return win[lo:lo + POOL_PAD] / cnt - xe[lo:lo + POOL_PAD]

            out = jnp.concatenate([edge(POOL_PAD), inner[2 * POOL_PAD:seq], edge(seq)], axis=0)
            pm_ref[:, tile * LANES:(tile + 1) * LANES] = out.astype(BF16)

    sub = tm // PROJ_SUBTILES
    for s in range(PROJ_SUBTILES):
        sp = slice(s * sub, (s + 1) * sub)
        r0 = pl.multiple_of(i * tm + s * sub, sub)
        f = (_dot(cl_ref[pl.ds(r0, sub), :], xcs_ref[0, :, 0:FFT_W])
             + _dot(nsl_ref[pl.ds(r0, sub), :], xcs_ref[0, :, FFT_W:2 * FFT_W]))
        yf = _dot(f.astype(BF16), fftw_ref[0])
        yp = _dot(pm_ref[pl.ds(r0, sub), :], poolw_ref[0]) * pscale_ref[0]
        y = (_dot_tn(oT_ref[0, :, sp], wout_ref[0, 0:ATTN_W, :])
             + _dot(yp.astype(BF16), wout_ref[0, ATTN_W:ATTN_W + POOL_W, :])
             + _dot(yf.astype(BF16), wout_ref[0, ATTN_W + POOL_W:, :]))
        o_ref[0, sp, :] = x_ref[0, sp, :] + gate_ref[0, 0, 0] * y


def _mix(l_arr, x, mod, per_batch, ctx_row, oT, up, xcs, cl, nsl, fftw, poolbd, pscale, wout):
    b_, seq, d = x.shape
    tm = min(ROW_TILE, seq)
    return pl.pallas_call(
        functools.partial(_mix_kernel, seq=seq, tm=tm),
        out_shape=jax.ShapeDtypeStruct(x.shape, F32),
        grid_spec=pltpu.PrefetchScalarGridSpec(
            num_scalar_prefetch=1,
            grid=(b_, seq // tm),
            in_specs=[pl.BlockSpec((1, tm, d), lambda b, i, l: (b, i, 0)),
                      _mod_spec(5, per_batch, ctx_row)(d),
                      pl.BlockSpec((1, ATTN_W, tm), lambda b, i, l: (b, 0, i)),
                      pl.BlockSpec((1, seq, POOL_W), lambda b, i, l: (b, 0, 0)),
                      pl.BlockSpec((1, seq, 2 * FFT_W), lambda b, i, l: (b, 0, 0)),
                      _resident((seq, seq), lambda b, i, l: (0, 0)),
                      _resident((seq, seq), lambda b, i, l: (0, 0)),
                      pl.BlockSpec((1, FFT_W, FFT_W), lambda b, i, l: (l[0], 0, 0)),
                      pl.BlockSpec((1, POOL_W, POOL_W), lambda b, i, l: (l[0], 0, 0)),
                      pl.BlockSpec((1, 1, POOL_W), lambda b, i, l: (l[0], 0, 0)),
                      pl.BlockSpec((1, d, d), lambda b, i, l: (l[0], 0, 0))],
            out_specs=pl.BlockSpec((1, tm, d), lambda b, i, l: (b, i, 0)),
            scratch_shapes=[pltpu.VMEM((seq + 2 * POOL_PAD, POOL_W), F32),
                            pltpu.VMEM((seq, POOL_W), BF16)]),
        compiler_params=_params(2),
        name="mix",
    )(l_arr, x, mod, oT, up, xcs, cl, nsl, fftw, poolbd, pscale, wout)


def _rope_tables(length):
    t = jnp.arange(length)
    n_freq = HEAD_DIM // 4
    inv_freq = ROPE_THETA ** (-jnp.arange(n_freq, dtype=F32) / n_freq)
    row = (t // GRID_W).astype(F32)
    col = (t % GRID_W).astype(F32)
    ang = jnp.concatenate([inv_freq[:, None] * row[None, :], inv_freq[:, None] * col[None, :]], axis=0)
    return jnp.cos(ang), jnp.sin(ang)


def _dft(n):
    k = jnp.arange(n)
    ang = ((k[:, None] * k[None, :]) % n).astype(F32) * (2.0 * math.pi / n)
    return jnp.cos(ang) * n ** -0.5, jnp.sin(ang) * n ** -0.5


def _block_diag(blocks):
    g, w, _ = blocks.shape
    out = jnp.zeros((g * w, g * w), blocks.dtype)
    for i in range(g):
        out = out.at[i * w:(i + 1) * w, i * w:(i + 1) * w].set(blocks[i])
    return out


def kernel(x, c, ctx, c_ctx, ada_w, ada_b, ffn1_w_gu, ffn1_w_down, w_in, q_gain, k_gain, pool_w, pool_scale,
           fft_w, w_out, ffn2_w_gu, ffn2_w_down, final_gain):
    bsz, seq, d = x.shape
    clen = ctx.shape[1]
    depth = ada_w.shape[0]

    rows = -(-(bsz + 1) // 8) * 8
    cc = jnp.zeros((rows, d), F32).at[:bsz].set(c).at[bsz].set(c_ctx)
    mod = _ada(cc, ada_w, ada_b).reshape(depth, N_MOD, rows, 1, d)

    w1gu, w1d = ffn1_w_gu.astype(BF16), ffn1_w_down.astype(BF16)
    w2gu, w2d = ffn2_w_gu.astype(BF16), ffn2_w_down.astype(BF16)
    wqkvT = jnp.swapaxes(w_in[:, :, :QKV_W], 1, 2).astype(BF16)
    wpf = w_in[:, :, QKV_W:].astype(BF16)
    qg = q_gain.reshape(depth, HEAD_DIM, 1)
    kg = k_gain.reshape(depth, HEAD_DIM, 1)
    poolbd = jax.vmap(_block_diag)(pool_w).astype(BF16)
    pscale = pool_scale.reshape(depth, 1, POOL_W)
    fftw = fft_w.astype(BF16)
    wout = w_out.astype(BF16)
    fg = final_gain.reshape(1, d)

    cosT, sinT = _rope_tables(seq)
    ones = jnp.ones((HALF, clen), F32)
    cc_, sc_ = _dft(FFT_GROUP_W)
    dftc = jnp.concatenate([_block_diag(jnp.stack([cc_] * N_FFT_GROUPS)),
                            _block_diag(jnp.stack([sc_] * N_FFT_GROUPS))], axis=1).astype(BF16)
    cl_x, sl_x = _dft(seq)
    cl_c, sl_c = _dft(clen)
    cl_x, nsl_x = cl_x.astype(BF16), (-sl_x).astype(BF16)
    cl_c, nsl_c = cl_c.astype(BF16), (-sl_c).astype(BF16)
    cosT, sinT, dftc, cl_x, nsl_x, cl_c, nsl_c = lax.optimization_barrier(
        (cosT, sinT, dftc, cl_x, nsl_x, cl_c, nsl_c))

    def ffn_x(l_arr, xv, ks, wgu, wd, final=False):
        return _ffn(l_arr, xv, mod, ks, True, bsz, wgu, wd, fg, final)

    def ffn_c(l_arr, cv, ks, wgu, wd):
        flat = cv.reshape(1, bsz * clen, d)
        return _ffn(l_arr, flat, mod, ks, False, bsz, wgu, wd, fg, False).reshape(bsz, clen, d)

    def first_half_step(l_arr, xv, cv):
        return ffn_x(l_arr, xv, (0, 1, 2), w1gu, w1d), ffn_c(l_arr, cv, (0, 1, 2), w1gu, w1d)

    def rest_of_layer(l_arr, xv, cv, last):
        qT, k, vT, up, xcs = _inproj(l_arr, xv, mod, True, bsz, wqkvT, wpf, qg, kg, cosT, sinT, dftc, True)
        qcT, kc, vcT, upc, xcsc = _inproj(l_arr, cv, mod, False, bsz, wqkvT, wpf, qg, kg, ones, ones, dftc, False)
        oT = _attn(qT, [k, kc], [vT, vcT])
        xv = _mix(l_arr, xv, mod, True, bsz, oT, up, xcs, cl_x, nsl_x, fftw, poolbd, pscale, wout)
        xv = ffn_x(l_arr, xv, (6, 7, 8), w2gu, w2d, final=last)
        if not last:
            ocT = _attn(qcT, [kc], [vcT])
            cv = _mix(l_arr, cv, mod, False, bsz, ocT, upc, xcsc, cl_c, nsl_c, fftw, poolbd, pscale, wout)
            cv = ffn_c(l_arr, cv, (6, 7, 8), w2gu, w2d)
        return xv, cv

    def body(carry, l):
        xv, cv = rest_of_layer(l.reshape(1), carry[0], carry[1], False)
        return first_half_step(l.reshape(1) + 1, xv, cv), None

    carry = first_half_step(jnp.zeros((1,), jnp.int32), x, ctx)
    (x, ctx), _ = lax.scan(body, carry, jnp.arange(depth - 1, dtype=jnp.int32))
    x, _ = rest_of_layer(jnp.full((1,), depth - 1, jnp.int32), x, ctx, True)
    return x
```

```python
import functools
import math

import jax
import jax.numpy as jnp
from jax import lax
from jax.experimental import pallas as pl
from jax.experimental.pallas import tpu as pltpu

F32 = jnp.float32
BF16 = jnp.bfloat16

GRID_W = 64
HEAD_DIM = 64
HALF = HEAD_DIM // 2
N_Q_HEADS = 8
N_KV_HEADS = 2
Q_PER_KV = N_Q_HEADS // N_KV_HEADS
ATTN_W = N_Q_HEADS * HEAD_DIM
KV_W = N_KV_HEADS * HEAD_DIM
QKV_W = ATTN_W + 2 * KV_W
POOL_W = 256
POOL_HALF_WINDOWS = (1, 2, 4, 8)
POOL_GROUP_W = POOL_W // len(POOL_HALF_WINDOWS)
POOL_PAD = 8
LANES = 128
FFT_W = 256
N_FFT_GROUPS = 4
FFT_GROUP_W = FFT_W // N_FFT_GROUPS
N_MOD = 9
ROPE_THETA = 10000.0
EPS = 1e-6
Q_SCALE = HEAD_DIM ** -0.5 * math.log2(math.e)

V7X_VMEM_LIMIT = 56 * 1024 * 1024
ROW_TILE = 1024
Q_TILE = 1024
Q_SUB = 512
KEY_CHUNK = 256
QK_LOOKAHEAD = 2
BF16_SUBLANES = 16
FFN_SUB_ROWS = 256
PROJ_SUBTILES = 1


def _dot(a, b):
    return jnp.dot(a, b, preferred_element_type=F32)


def _dot_nt(a, b):
    return lax.dot_general(a, b, (((1,), (1,)), ((), ())), preferred_element_type=F32)


def _dot_tn(a, b):
    return lax.dot_general(a, b, (((0,), (0,)), ((), ())), preferred_element_type=F32)


def _rms(x):
    return x * lax.rsqrt(jnp.mean(x * x, axis=-1, keepdims=True) + EPS)


def _resident(shape, index_map):
    return pl.BlockSpec(shape, index_map, pipeline_mode=pl.Buffered(1))


def _params(n_axes):
    return pltpu.CompilerParams(dimension_semantics=("arbitrary",) * n_axes,
                                vmem_limit_bytes=V7X_VMEM_LIMIT)


def _ada_kernel(cc_ref, w_ref, b_ref, o_ref):
    a = cc_ref[...]
    a = a / (1.0 + jnp.exp(-a))
    a_hi = a.astype(BF16)
    a_lo = (a - a_hi.astype(F32)).astype(BF16)
    w = w_ref[0].astype(BF16)
    o_ref[0, 0] = _dot(a_hi, w) + _dot(a_lo, w) + b_ref[0]


def _ada(cc, ada_w, ada_b):
    depth, d, _ = ada_w.shape
    rows = cc.shape[0]
    return pl.pallas_call(
        _ada_kernel,
        out_shape=jax.ShapeDtypeStruct((depth, N_MOD, rows, d), F32),
        grid=(depth, N_MOD),
        in_specs=[pl.BlockSpec((rows, d), lambda l, n: (0, 0)),
                  pl.BlockSpec((1, d, d), lambda l, n: (l, 0, n)),
                  pl.BlockSpec((1, 1, d), lambda l, n: (l, 0, n))],
        out_specs=pl.BlockSpec((1, 1, rows, d), lambda l, n: (l, n, 0, 0)),
        compiler_params=_params(2),
        name="ada_mod",
    )(cc, ada_w, ada_b.reshape(depth, 1, N_MOD * d))


def _mod_spec(k, per_batch, ctx_row):
    if per_batch:
        return lambda d: pl.BlockSpec((1, 1, 1, 1, d), lambda b, i, l: (l[0], k, b, 0, 0))
    return lambda d: pl.BlockSpec((1, 1, 1, 1, d), lambda b, i, l: (l[0], k, ctx_row, 0, 0))


def _ffn_kernel(l_ref, x_ref, shift_ref, scale_ref, gate_ref, wgu_ref, wd_ref, fg_ref, o_ref, *, final):
    tm = x_ref.shape[1]
    sub = min(FFN_SUB_ROWS, tm)
    f = wd_ref.shape[1]
    xs = [x_ref[0, r:r + sub, :] for r in range(0, tm, sub)]
    hs = [(_rms(x) * (1.0 + scale_ref[0, 0, 0]) + shift_ref[0, 0, 0]).astype(BF16) for x in xs]
    gus = [_dot(h, wgu_ref[0]) for h in hs]
    acts = [(gu[:, :f] / (1.0 + jnp.exp(-gu[:, :f])) * gu[:, f:]).astype(BF16) for gu in gus]
    ys = [_dot(a, wd_ref[0]) for a in acts]
    for s, (x, y) in enumerate(zip(xs, ys)):
        out = x + (0.5 * gate_ref[0, 0, 0]) * y
        if final:
            out = _rms(out) * fg_ref[...]
        o_ref[0, s * sub:(s + 1) * sub, :] = out


def _ffn(l_arr, x, mod, mod_k, per_batch, ctx_row, wgu, wd, final_gain, final):
    bx, rows, d = x.shape
    tm = min(ROW_TILE, rows)
    f2 = wgu.shape[2]
    specs = [_mod_spec(k, per_batch, ctx_row)(d) for k in mod_k]
    return pl.pallas_call(
        functools.partial(_ffn_kernel, final=final),
        out_shape=jax.ShapeDtypeStruct(x.shape, F32),
        grid_spec=pltpu.PrefetchScalarGridSpec(
            num_scalar_prefetch=1,
            grid=(bx, rows // tm),
            in_specs=[pl.BlockSpec((1, tm, d), lambda b, i, l: (b, i, 0))] + specs + [
                _resident((1, d, f2), lambda b, i, l: (l[0], 0, 0)),
                _resident((1, f2 // 2, d), lambda b, i, l: (l[0], 0, 0)),
                pl.BlockSpec((1, d), lambda b, i, l: (0, 0))],
            out_specs=pl.BlockSpec((1, tm, d), lambda b, i, l: (b, i, 0))),
        compiler_params=_params(2),
        name="ffn_final" if final else "ffn",
    )(l_arr, x, mod, mod, mod, wgu, wd, final_gain)


def _inproj_kernel(l_ref, x_ref, shift_ref, scale_ref, wqkv_ref, wpf_ref, qg_ref, kg_ref, cos_ref, sin_ref,
                   dftc_ref, qT_ref, k_ref, vT_ref, up_ref, xcs_ref, *, rope):
    sub = x_ref.shape[1] // PROJ_SUBTILES
    spans = [slice(s * sub, (s + 1) * sub) for s in range(PROJ_SUBTILES)]
    hs = [(_rms(x_ref[0, sp, :]) * (1.0 + scale_ref[0, 0, 0]) + shift_ref[0, 0, 0]).astype(BF16) for sp in spans]
    zTs = [_dot_nt(wqkv_ref[0], h) for h in hs]
    z2s = [_dot(h, wpf_ref[0]) for h in hs]

    def head(zh, gain, sp):
        zn = zh * lax.rsqrt(jnp.mean(zh * zh, axis=0, keepdims=True) + EPS) * gain
        x1, x2 = zn[:HALF], zn[HALF:]
        if not rope:
            return x1, x2
        c, s = cos_ref[:, sp], sin_ref[:, sp]
        return x1 * c - x2 * s, x1 * s + x2 * c

    for sp, zT, z2 in zip(spans, zTs, z2s):
        for j in range(N_Q_HEADS):
            r = j * HEAD_DIM
            a, b = head(zT[r:r + HEAD_DIM], qg_ref[0], sp)
            qT_ref[0, r:r + HALF, sp] = (a * Q_SCALE).astype(BF16)
            qT_ref[0, r + HALF:r + HEAD_DIM, sp] = (b * Q_SCALE).astype(BF16)
        parts = []
        for j in range(N_KV_HEADS):
            r = ATTN_W + j * HEAD_DIM
            parts.extend(head(zT[r:r + HEAD_DIM], kg_ref[0], sp))
        k_ref[0, sp, :] = jnp.concatenate(parts, axis=0).T.astype(BF16)
        vT_ref[0, :, sp] = zT[ATTN_W + KV_W:].astype(BF16)
        up_ref[0, sp, :] = z2[:, :POOL_W]
        xcs_ref[0, sp, :] = _dot(z2[:, POOL_W:].astype(BF16), dftc_ref[...]).astype(BF16)


def _inproj(l_arr, x, mod, per_batch, ctx_row, wqkvT, wpf, qg, kg, cosT, sinT, dftc, rope):
    b_, rows, d = x.shape
    tm = min(ROW_TILE, rows)
    pf = POOL_W + FFT_W
    specs = [_mod_spec(k, per_batch, ctx_row)(d) for k in (3, 4)]
    out_shape = (jax.ShapeDtypeStruct((b_, ATTN_W, rows), BF16),
                 jax.ShapeDtypeStruct((b_, rows, KV_W), BF16),
                 jax.ShapeDtypeStruct((b_, KV_W, rows), BF16),
                 jax.ShapeDtypeStruct((b_, rows, POOL_W), F32),
                 jax.ShapeDtypeStruct((b_, rows, 2 * FFT_W), BF16))
    return pl.pallas_call(
        functools.partial(_inproj_kernel, rope=rope),
        out_shape=out_shape,
        grid_spec=pltpu.PrefetchScalarGridSpec(
            num_scalar_prefetch=1,
            grid=(b_, rows // tm),
            in_specs=[pl.BlockSpec((1, tm, d), lambda b, i, l: (b, i, 0))] + specs + [
                pl.BlockSpec((1, QKV_W, d), lambda b, i, l: (l[0], 0, 0)),
                pl.BlockSpec((1, d, pf), lambda b, i, l: (l[0], 0, 0)),
                pl.BlockSpec((1, HEAD_DIM, 1), lambda b, i, l: (l[0], 0, 0)),
                pl.BlockSpec((1, HEAD_DIM, 1), lambda b, i, l: (l[0], 0, 0)),
                pl.BlockSpec((HALF, tm), lambda b, i, l: (0, i)),
                pl.BlockSpec((HALF, tm), lambda b, i, l: (0, i)),
                pl.BlockSpec((FFT_W, 2 * FFT_W), lambda b, i, l: (0, 0))],
            out_specs=[pl.BlockSpec((1, ATTN_W, tm), lambda b, i, l: (b, 0, i)),
                       pl.BlockSpec((1, tm, KV_W), lambda b, i, l: (b, i, 0)),
                       pl.BlockSpec((1, KV_W, tm), lambda b, i, l: (b, 0, i)),
                       pl.BlockSpec((1, tm, POOL_W), lambda b, i, l: (b, i, 0)),
                       pl.BlockSpec((1, tm, 2 * FFT_W), lambda b, i, l: (b, i, 0))]),
        compiler_params=_params(2),
        name="in_proj_rope" if rope else "in_proj",
    )(l_arr, x, mod, mod, wqkvT, wpf, qg, kg, cosT, sinT, dftc)


def _attn_kernel(*refs, n_seg, key_chunk):
    qT_ref, k_refs, vT_refs, o_ref = refs[0], refs[1:1 + n_seg], refs[1 + n_seg:1 + 2 * n_seg], refs[-1]
    chunks = [(k_ref, vT_ref, r) for k_ref, vT_ref in zip(k_refs, vT_refs)
              for r in range(0, k_ref.shape[1], key_chunk)]
    q_sub = min(Q_SUB, qT_ref.shape[2])
    steps = [(q0, j, c) for q0 in range(0, qT_ref.shape[2], q_sub)
             for j in range(N_Q_HEADS) for c in range(len(chunks))]

    def scores(q0, j, c):
        k_ref, _, r = chunks[c]
        qj = qT_ref[0, j * HEAD_DIM:(j + 1) * HEAD_DIM, q0:q0 + q_sub]
        zero = jnp.zeros_like(qj)
        qb = jnp.concatenate([qj, zero] if j // Q_PER_KV == 0 else [zero, qj], axis=0)
        return _dot(k_ref[0, r:r + key_chunk, :], qb)

    pending = [scores(*st) for st in steps[:QK_LOOKAHEAD]]
    for n, (q0, j, c) in enumerate(steps):
        _, vT_ref, r = chunks[c]
        lo = (j // Q_PER_KV) * HEAD_DIM
        s = pending.pop(0)
        if n + QK_LOOKAHEAD < len(steps):
            pending.append(scores(*steps[n + QK_LOOKAHEAD]))
        cm = jnp.max(s, axis=0, keepdims=True)
        m_new = cm if c == 0 else jnp.maximum(m, cm)
        p = jnp.exp2(s - m_new)
        vaug = jnp.concatenate([vT_ref[0, lo:lo + HEAD_DIM, r:r + key_chunk],
                                jnp.ones((BF16_SUBLANES, key_chunk), BF16)], axis=0)
        pvf = _dot(vaug, p.astype(BF16))
        pv, ps = pvf[:HEAD_DIM], pvf[HEAD_DIM:HEAD_DIM + 1]
        if c == 0:
            den, acc = ps, pv
        else:
            alpha = jnp.exp2(m - m_new)
            den, acc = alpha * den + ps, alpha * acc + pv
        m = m_new
        if c == len(chunks) - 1:
            o_ref[0, j * HEAD_DIM:(j + 1) * HEAD_DIM, q0:q0 + q_sub] = (acc / den).astype(BF16)


def _attn(qT, ks, vTs):
    b_, _, rows = qT.shape
    tq = min(Q_TILE, rows)
    key_chunk = min([KEY_CHUNK] + [k.shape[1] for k in ks])
    assert all(k.shape[1] % key_chunk == 0 for k in ks)
    return pl.pallas_call(
        functools.partial(_attn_kernel, n_seg=len(ks), key_chunk=key_chunk),
        out_shape=jax.ShapeDtypeStruct(qT.shape, BF16),
        grid=(b_, rows // tq),
        in_specs=([pl.BlockSpec((1, ATTN_W, tq), lambda b, i: (b, 0, i))]
                  + [pl.BlockSpec((1, k.shape[1], KV_W), lambda b, i: (b, 0, 0)) for k in ks]
                  + [pl.BlockSpec((1, KV_W, v.shape[2]), lambda b, i: (b, 0, 0)) for v in vTs]),
        out_specs=pl.BlockSpec((1, ATTN_W, tq), lambda b, i: (b, 0, i)),
        compiler_params=_params(2),
        name="attn",
    )(qT, *ks, *vTs)


def _mix_kernel(l_ref, x_ref, gate_ref, oT_ref, up_ref, xcs_ref, cl_ref, nsl_ref, fftw_ref, poolw_ref, pscale_ref,
                wout_ref, o_ref, pad_ref, pm_ref, *, seq, tm):
    i = pl.program_id(1)

    @pl.when(i == 0)
    def _():
        n = seq + 2 * POOL_PAD
        zeros = jnp.zeros((POOL_PAD, POOL_W), F32)
        pad_ref[0:POOL_PAD, :] = zeros
        pad_ref[POOL_PAD + seq:n, :] = zeros
        pad_ref[POOL_PAD:POOL_PAD + seq, :] = up_ref[0]
        first_half = lax.broadcasted_iota(jnp.int32, (1, LANES), 1) < POOL_GROUP_W
        for tile in range(POOL_W // LANES):
            xe = pad_ref[:, tile * LANES:(tile + 1) * LANES]
            hws = POOL_HALF_WINDOWS[2 * tile:2 * tile + 2]
            fwd, w, centred = xe, 1, []
            while w < 2 * hws[-1]:
                fwd = fwd + pltpu.roll(fwd, n - w, axis=0)
                w *= 2
                if w // 2 in hws:
                    centred.append(pltpu.roll(fwd, w // 2, axis=0))
            win = jnp.where(first_half, centred[0], centred[1])
            hw = jnp.where(first_half, hws[0], hws[1])
            inner = win * (0.5 / hw.astype(F32)) - xe

            def edge(lo):
                t = lax.broadcasted_iota(jnp.int32, (POOL_PAD, LANES), 0) + (lo - POOL_PAD)
                cnt = (jnp.minimum(t + hw, seq) - jnp.maximum(t - hw, 0)).astype(F32)
                return win[lo:lo + POOL_PAD] / cnt - xe[lo:lo + POOL_PAD]

            out = jnp.concatenate([edge(POOL_PAD), inner[2 * POOL_PAD:seq], edge(seq)], axis=0)
            pm_ref[:, tile * LANES:(tile + 1) * LANES] = out.astype(BF16)

    sub = tm // PROJ_SUBTILES
    for s in range(PROJ_SUBTILES):
        sp = slice(s * sub, (s + 1) * sub)
        r0 = pl.multiple_of(i * tm + s * sub, sub)
        f = (_dot(cl_ref[pl.ds(r0, sub), :], xcs_ref[0, :, 0:FFT_W])
             + _dot(nsl_ref[pl.ds(r0, sub), :], xcs_ref[0, :, FFT_W:2 * FFT_W]))
        yf = _dot(f.astype(BF16), fftw_ref[0])
        yp = _dot(pm_ref[pl.ds(r0, sub), :], poolw_ref[0]) * pscale_ref[0]
        y = (_dot_tn(oT_ref[0, :, sp], wout_ref[0, 0:ATTN_W, :])
             + _dot(yp.astype(BF16), wout_ref[0, ATTN_W:ATTN_W + POOL_W, :])
             + _dot(yf.astype(BF16), wout_ref[0, ATTN_W + POOL_W:, :]))
        o_ref[0, sp, :] = x_ref[0, sp, :] + gate_ref[0, 0, 0] * y


def _mix(l_arr, x, mod, per_batch, ctx_row, oT, up, xcs, cl, nsl, fftw, poolbd, pscale, wout):
    b_, seq, d = x.shape
    tm = min(ROW_TILE, seq)
    return pl.pallas_call(
        functools.partial(_mix_kernel, seq=seq, tm=tm),
        out_shape=jax.ShapeDtypeStruct(x.shape, F32),
        grid_spec=pltpu.PrefetchScalarGridSpec(
            num_scalar_prefetch=1,
            grid=(b_, seq // tm),
            in_specs=[pl.BlockSpec((1, tm, d), lambda b, i, l: (b, i, 0)),
                      _mod_spec(5, per_batch, ctx_row)(d),
                      pl.BlockSpec((1, ATTN_W, tm), lambda b, i, l: (b, 0, i)),
                      pl.BlockSpec((1, seq, POOL_W), lambda b, i, l: (b, 0, 0)),
                      pl.BlockSpec((1, seq, 2 * FFT_W), lambda b, i, l: (b, 0, 0)),
                      _resident((seq, seq), lambda b, i, l: (0, 0)),
                      _resident((seq, seq), lambda b, i, l: (0, 0)),
                      pl.BlockSpec((1, FFT_W, FFT_W), lambda b, i, l: (l[0], 0, 0)),
                      pl.BlockSpec((1, POOL_W, POOL_W), lambda b, i, l: (l[0], 0, 0)),
                      pl.BlockSpec((1, 1, POOL_W), lambda b, i, l: (l[0], 0, 0)),
                      pl.BlockSpec((1, d, d), lambda b, i, l: (l[0], 0, 0))],
            out_specs=pl.BlockSpec((1, tm, d), lambda b, i, l: (b, i, 0)),
            scratch_shapes=[pltpu.VMEM((seq + 2 * POOL_PAD, POOL_W), F32),
                            pltpu.VMEM((seq, POOL_W), BF16)]),
        compiler_params=_params(2),
        name="mix",
    )(l_arr, x, mod, oT, up, xcs, cl, nsl, fftw, poolbd, pscale, wout)


def _rope_tables(length):
    t = jnp.arange(length)
    n_freq = HEAD_DIM // 4
    inv_freq = ROPE_THETA ** (-jnp.arange(n_freq, dtype=F32) / n_freq)
    row = (t // GRID_W).astype(F32)
    col = (t % GRID_W).astype(F32)
    ang = jnp.concatenate([inv_freq[:, None] * row[None, :], inv_freq[:, None] * col[None, :]], axis=0)
    return jnp.cos(ang), jnp.sin(ang)


def _dft(n):
    inner = math.gcd(n, 64)
    k = jnp.arange(n)[:, None]

    def table(t):
        ang = ((k * t[None, :]) % n).astype(F32) * (2.0 * math.pi / n)
        return jnp.cos(ang), jnp.sin(ang)

    (ca, sa), (cb, sb) = table(jnp.arange(n // inner) * inner), table(jnp.arange(inner))
    ca, sa, cb, sb = ca[:, :, None], sa[:, :, None], cb[:, None, :], sb[:, None, :]
    scale = n ** -0.5
    return ((ca * cb - sa * sb) * scale).reshape(n, n), ((sa * cb + ca * sb) * scale).reshape(n, n)


def _block_diag(blocks):
    g, w, _ = blocks.shape
    out = jnp.zeros((g * w, g * w), blocks.dtype)
    for i in range(g):
        out = out.at[i * w:(i + 1) * w, i * w:(i + 1) * w].set(blocks[i])
    return out


def kernel(x, c, ctx, c_ctx, ada_w, ada_b, ffn1_w_gu, ffn1_w_down, w_in, q_gain, k_gain, pool_w, pool_scale,
           fft_w, w_out, ffn2_w_gu, ffn2_w_down, final_gain):
    bsz, seq, d = x.shape
    clen = ctx.shape[1]
    depth = ada_w.shape[0]

    rows = -(-(bsz + 1) // 8) * 8
    cc = jnp.zeros((rows, d), F32).at[:bsz].set(c).at[bsz].set(c_ctx)
    mod = _ada(cc, ada_w, ada_b).reshape(depth, N_MOD, rows, 1, d)

    w1gu, w1d = ffn1_w_gu.astype(BF16), ffn1_w_down.astype(BF16)
    w2gu, w2d = ffn2_w_gu.astype(BF16), ffn2_w_down.astype(BF16)
    wqkvT = jnp.swapaxes(w_in[:, :, :QKV_W], 1, 2).astype(BF16)
    wpf = w_in[:, :, QKV_W:].astype(BF16)
    qg = q_gain.reshape(depth, HEAD_DIM, 1)
    kg = k_gain.reshape(depth, HEAD_DIM, 1)
    poolbd = jax.vmap(_block_diag)(pool_w).astype(BF16)
    pscale = pool_scale.reshape(depth, 1, POOL_W)
    fftw = fft_w.astype(BF16)
    wout = w_out.astype(BF16)
    fg = final_gain.reshape(1, d)

    cosT, sinT = _rope_tables(seq)
    ones = jnp.ones((HALF, clen), F32)
    cc_, sc_ = _dft(FFT_GROUP_W)
    dftc = jnp.concatenate([_block_diag(jnp.stack([cc_] * N_FFT_GROUPS)),
                            _block_diag(jnp.stack([sc_] * N_FFT_GROUPS))], axis=1).astype(BF16)
    cl_x, sl_x = _dft(seq)
    cl_c, sl_c = _dft(clen)
    cl_x, nsl_x = cl_x.astype(BF16), (-sl_x).astype(BF16)
    cl_c, nsl_c = cl_c.astype(BF16), (-sl_c).astype(BF16)
    cosT, sinT, dftc, cl_x, nsl_x, cl_c, nsl_c = lax.optimization_barrier(
        (cosT, sinT, dftc, cl_x, nsl_x, cl_c, nsl_c))

    def ffn_x(l_arr, xv, ks, wgu, wd, final=False):
        return _ffn(l_arr, xv, mod, ks, True, bsz, wgu, wd, fg, final)

    def ffn_c(l_arr, cv, ks, wgu, wd):
        flat = cv.reshape(1, bsz * clen, d)
        return _ffn(l_arr, flat, mod, ks, False, bsz, wgu, wd, fg, False).reshape(bsz, clen, d)

    def first_half_step(l_arr, xv, cv):
        return ffn_x(l_arr, xv, (0, 1, 2), w1gu, w1d), ffn_c(l_arr, cv, (0, 1, 2), w1gu, w1d)

    def rest_of_layer(l_arr, xv, cv, last):
        qT, k, vT, up, xcs = _inproj(l_arr, xv, mod, True, bsz, wqkvT, wpf, qg, kg, cosT, sinT, dftc, True)
        qcT, kc, vcT, upc, xcsc = _inproj(l_arr, cv, mod, False, bsz, wqkvT, wpf, qg, kg, ones, ones, dftc, False)
        oT = _attn(qT, [k, kc], [vT, vcT])
        xv = _mix(l_arr, xv, mod, True, bsz, oT, up, xcs, cl_x, nsl_x, fftw, poolbd, pscale, wout)
        xv = ffn_x(l_arr, xv, (6, 7, 8), w2gu, w2d, final=last)
        if not last:
            ocT = _attn(qcT, [kc], [vcT])
            cv = _mix(l_arr, cv, mod, False, bsz, ocT, upc, xcsc, cl_c, nsl_c, fftw, poolbd, pscale, wout)
            cv = ffn_c(l_arr, cv, (6, 7, 8), w2gu, w2d)
        return xv, cv

    def body(carry, l):
        xv, cv = rest_of_layer(l.reshape(1), carry[0], carry[1], False)
        return first_half_step(l.reshape(1) + 1, xv, cv), None

    carry = first_half_step(jnp.zeros((1,), jnp.int32), x, ctx)
    (x, ctx), _ = lax.scan(body, carry, jnp.arange(depth - 1, dtype=jnp.int32))
    x, _ = rest_of_layer(jnp.full((1,), depth - 1, jnp.int32), x, ctx, True)
    return x
```

```python
import functools
import math

import jax
import jax.numpy as jnp
from jax import lax
from jax.experimental import pallas as pl
from jax.experimental.pallas import tpu as pltpu

F32 = jnp.float32
BF16 = jnp.bfloat16

GRID_W = 64
HEAD_DIM = 64
HALF = HEAD_DIM // 2
N_Q_HEADS = 8
N_KV_HEADS = 2
Q_PER_KV = N_Q_HEADS // N_KV_HEADS
ATTN_W = N_Q_HEADS * HEAD_DIM
KV_W = N_KV_HEADS * HEAD_DIM
QKV_W = ATTN_W + 2 * KV_W
POOL_W = 256
POOL_HALF_WINDOWS = (1, 2, 4, 8)
POOL_GROUP_W = POOL_W // len(POOL_HALF_WINDOWS)
POOL_PAD = 8
LANES = 128
FFT_W = 256
N_FFT_GROUPS = 4
FFT_GROUP_W = FFT_W // N_FFT_GROUPS
N_MOD = 9
ROPE_THETA = 10000.0
EPS = 1e-6
Q_SCALE = HEAD_DIM ** -0.5 * math.log2(math.e)

V7X_VMEM_LIMIT = 56 * 1024 * 1024
ROW_TILE = 1024
PROJ_ROW_TILE = 2048
Q_TILE = 1024
Q_SUB = 512
KEY_CHUNK = 256
QK_LOOKAHEAD = 2
BF16_SUBLANES = 16
FFN_SUB_ROWS = 256
PROJ_SUBTILES = 1


def _dot(a, b):
    return jnp.dot(a, b, preferred_element_type=F32)


def _dot_nt(a, b):
    return lax.dot_general(a, b, (((1,), (1,)), ((), ())), preferred_element_type=F32)


def _dot_tn(a, b):
    return lax.dot_general(a, b, (((0,), (0,)), ((), ())), preferred_element_type=F32)


def _rms(x):
    return x * lax.rsqrt(jnp.mean(x * x, axis=-1, keepdims=True) + EPS)


def _resident(shape, index_map):
    return pl.BlockSpec(shape, index_map, pipeline_mode=pl.Buffered(1))


def _params(n_axes):
    return pltpu.CompilerParams(dimension_semantics=("arbitrary",) * n_axes,
                                vmem_limit_bytes=V7X_VMEM_LIMIT)


def _ada_kernel(cc_ref, w_ref, b_ref, o_ref):
    a = cc_ref[...]
    a = a / (1.0 + jnp.exp(-a))
    a_hi = a.astype(BF16)
    a_lo = (a - a_hi.astype(F32)).astype(BF16)
    w = w_ref[0].astype(BF16)
    o_ref[0, 0] = _dot(a_hi, w) + _dot(a_lo, w) + b_ref[0]


def _ada(cc, ada_w, ada_b):
    depth, d, _ = ada_w.shape
    rows = cc.shape[0]
    return pl.pallas_call(
        _ada_kernel,
        out_shape=jax.ShapeDtypeStruct((depth, N_MOD, rows, d), F32),
        grid=(depth, N_MOD),
        in_specs=[pl.BlockSpec((rows, d), lambda l, n: (0, 0)),
                  pl.BlockSpec((1, d, d), lambda l, n: (l, 0, n)),
                  pl.BlockSpec((1, 1, d), lambda l, n: (l, 0, n))],
        out_specs=pl.BlockSpec((1, 1, rows, d), lambda l, n: (l, n, 0, 0)),
        compiler_params=_params(2),
        name="ada_mod",
    )(cc, ada_w, ada_b.reshape(depth, 1, N_MOD * d))


def _mod_spec(k, per_batch, ctx_row):
    if per_batch:
        return lambda d: pl.BlockSpec((1, 1, 1, 1, d), lambda b, i, l: (l[0], k, b, 0, 0))
    return lambda d: pl.BlockSpec((1, 1, 1, 1, d), lambda b, i, l: (l[0], k, ctx_row, 0, 0))


def _ffn_kernel(l_ref, x_ref, shift_ref, scale_ref, gate_ref, wgu_ref, wd_ref, fg_ref, o_ref, *, final):
    tm = x_ref.shape[1]
    sub = min(FFN_SUB_ROWS, tm)
    f = wd_ref.shape[1]
    xs = [x_ref[0, r:r + sub, :] for r in range(0, tm, sub)]
    hs = [(_rms(x) * (1.0 + scale_ref[0, 0, 0]) + shift_ref[0, 0, 0]).astype(BF16) for x in xs]
    gus = [_dot(h, wgu_ref[0]) for h in hs]
    acts = [(gu[:, :f] / (1.0 + jnp.exp(-gu[:, :f])) * gu[:, f:]).astype(BF16) for gu in gus]
    ys = [_dot(a, wd_ref[0]) for a in acts]
    for s, (x, y) in enumerate(zip(xs, ys)):
        out = x + (0.5 * gate_ref[0, 0, 0]) * y
        if final:
            out = _rms(out) * fg_ref[...]
        o_ref[0, s * sub:(s + 1) * sub, :] = out


def _ffn(l_arr, x, mod, mod_k, per_batch, ctx_row, wgu, wd, final_gain, final):
    bx, rows, d = x.shape
    tm = min(ROW_TILE, rows)
    f2 = wgu.shape[2]
    specs = [_mod_spec(k, per_batch, ctx_row)(d) for k in mod_k]
    return pl.pallas_call(
        functools.partial(_ffn_kernel, final=final),
        out_shape=jax.ShapeDtypeStruct(x.shape, F32),
        grid_spec=pltpu.PrefetchScalarGridSpec(
            num_scalar_prefetch=1,
            grid=(bx, rows // tm),
            in_specs=[pl.BlockSpec((1, tm, d), lambda b, i, l: (b, i, 0))] + specs + [
                _resident((1, d, f2), lambda b, i, l: (l[0], 0, 0)),
                _resident((1, f2 // 2, d), lambda b, i, l: (l[0], 0, 0)),
                pl.BlockSpec((1, d), lambda b, i, l: (0, 0))],
            out_specs=pl.BlockSpec((1, tm, d), lambda b, i, l: (b, i, 0))),
        compiler_params=_params(2),
        name="ffn_final" if final else "ffn",
    )(l_arr, x, mod, mod, mod, wgu, wd, final_gain)


def _inproj_kernel(l_ref, x_ref, shift_ref, scale_ref, wqkv_ref, wpf_ref, qg_ref, kg_ref, cos_ref, sin_ref,
                   dftc_ref, qT_ref, k_ref, vT_ref, up_ref, xcs_ref, *, rope):
    sub = x_ref.shape[1] // PROJ_SUBTILES
    spans = [slice(s * sub, (s + 1) * sub) for s in range(PROJ_SUBTILES)]
    hs = [(_rms(x_ref[0, sp, :]) * (1.0 + scale_ref[0, 0, 0]) + shift_ref[0, 0, 0]).astype(BF16) for sp in spans]
    zTs = [_dot_nt(wqkv_ref[0], h) for h in hs]
    z2s = [_dot(h, wpf_ref[0]) for h in hs]

    def head(zh, gain, sp):
        zn = zh * lax.rsqrt(jnp.mean(zh * zh, axis=0, keepdims=True) + EPS) * gain
        x1, x2 = zn[:HALF], zn[HALF:]
        if not rope:
            return x1, x2
        c, s = cos_ref[:, sp], sin_ref[:, sp]
        return x1 * c - x2 * s, x1 * s + x2 * c

    for sp, zT, z2 in zip(spans, zTs, z2s):
        for j in range(N_Q_HEADS):
            r = j * HEAD_DIM
            a, b = head(zT[r:r + HEAD_DIM], qg_ref[0], sp)
            qT_ref[0, r:r + HALF, sp] = (a * Q_SCALE).astype(BF16)
            qT_ref[0, r + HALF:r + HEAD_DIM, sp] = (b * Q_SCALE).astype(BF16)
        parts = []
        for j in range(N_KV_HEADS):
            r = ATTN_W + j * HEAD_DIM
            parts.extend(head(zT[r:r + HEAD_DIM], kg_ref[0], sp))
        k_ref[0, sp, :] = jnp.concatenate(parts, axis=0).T.astype(BF16)
        vT_ref[0, :, sp] = zT[ATTN_W + KV_W:].astype(BF16)
        up_ref[0, sp, :] = z2[:, :POOL_W]
        xcs_ref[0, sp, :] = _dot(z2[:, POOL_W:].astype(BF16), dftc_ref[...]).astype(BF16)


def _inproj(l_arr, x, mod, per_batch, ctx_row, wqkvT, wpf, qg, kg, cosT, sinT, dftc, rope):
    b_, rows, d = x.shape
    tm = min(PROJ_ROW_TILE, rows)
    pf = POOL_W + FFT_W
    specs = [_mod_spec(k, per_batch, ctx_row)(d) for k in (3, 4)]
    out_shape = (jax.ShapeDtypeStruct((b_, ATTN_W, rows), BF16),
                 jax.ShapeDtypeStruct((b_, rows, KV_W), BF16),
                 jax.ShapeDtypeStruct((b_, KV_W, rows), BF16),
                 jax.ShapeDtypeStruct((b_, rows, POOL_W), F32),
                 jax.ShapeDtypeStruct((b_, rows, 2 * FFT_W), BF16))
    return pl.pallas_call(
        functools.partial(_inproj_kernel, rope=rope),
        out_shape=out_shape,
        grid_spec=pltpu.PrefetchScalarGridSpec(
            num_scalar_prefetch=1,
            grid=(b_, rows // tm),
            in_specs=[pl.BlockSpec((1, tm, d), lambda b, i, l: (b, i, 0))] + specs + [
                pl.BlockSpec((1, QKV_W, d), lambda b, i, l: (l[0], 0, 0)),
                pl.BlockSpec((1, d, pf), lambda b, i, l: (l[0], 0, 0)),
                pl.BlockSpec((1, HEAD_DIM, 1), lambda b, i, l: (l[0], 0, 0)),
                pl.BlockSpec((1, HEAD_DIM, 1), lambda b, i, l: (l[0], 0, 0)),
                pl.BlockSpec((HALF, tm), lambda b, i, l: (0, i)),
                pl.BlockSpec((HALF, tm), lambda b, i, l: (0, i)),
                pl.BlockSpec((FFT_W, 2 * FFT_W), lambda b, i, l: (0, 0))],
            out_specs=[pl.BlockSpec((1, ATTN_W, tm), lambda b, i, l: (b, 0, i)),
                       pl.BlockSpec((1, tm, KV_W), lambda b, i, l: (b, i, 0)),
                       pl.BlockSpec((1, KV_W, tm), lambda b, i, l: (b, 0, i)),
                       pl.BlockSpec((1, tm, POOL_W), lambda b, i, l: (b, i, 0)),
                       pl.BlockSpec((1, tm, 2 * FFT_W), lambda b, i, l: (b, i, 0))]),
        compiler_params=_params(2),
        name="in_proj_rope" if rope else "in_proj",
    )(l_arr, x, mod, mod, wqkvT, wpf, qg, kg, cosT, sinT, dftc)


def _attn_kernel(*refs, n_seg, key_chunk):
    qT_ref, k_refs, vT_refs, o_ref = refs[0], refs[1:1 + n_seg], refs[1 + n_seg:1 + 2 * n_seg], refs[-1]
    chunks = [(k_ref, vT_ref, r) for k_ref, vT_ref in zip(k_refs, vT_refs)
              for r in range(0, k_ref.shape[1], key_chunk)]
    q_sub = min(Q_SUB, qT_ref.shape[2])
    steps = [(q0, j, c) for q0 in range(0, qT_ref.shape[2], q_sub)
             for j in range(N_Q_HEADS) for c in range(len(chunks))]

    def scores(q0, j, c):
        k_ref, _, r = chunks[c]
        qj = qT_ref[0, j * HEAD_DIM:(j + 1) * HEAD_DIM, q0:q0 + q_sub]
        zero = jnp.zeros_like(qj)
        qb = jnp.concatenate([qj, zero] if j // Q_PER_KV == 0 else [zero, qj], axis=0)
        return _dot(k_ref[0, r:r + key_chunk, :], qb)

    pending = [scores(*st) for st in steps[:QK_LOOKAHEAD]]
    for n, (q0, j, c) in enumerate(steps):
        _, vT_ref, r = chunks[c]
        lo = (j // Q_PER_KV) * HEAD_DIM
        s = pending.pop(0)
        if n + QK_LOOKAHEAD < len(steps):
            pending.append(scores(*steps[n + QK_LOOKAHEAD]))
        cm = jnp.max(s, axis=0, keepdims=True)
        m_new = cm if c == 0 else jnp.maximum(m, cm)
        p = jnp.exp2(s - m_new)
        vaug = jnp.concatenate([vT_ref[0, lo:lo + HEAD_DIM, r:r + key_chunk],
                                jnp.ones((BF16_SUBLANES, key_chunk), BF16)], axis=0)
        pvf = _dot(vaug, p.astype(BF16))
        pv, ps = pvf[:HEAD_DIM], pvf[HEAD_DIM:HEAD_DIM + 1]
        if c == 0:
            den, acc = ps, pv
        else:
            alpha = jnp.exp2(m - m_new)
            den, acc = alpha * den + ps, alpha * acc + pv
        m = m_new
        if c == len(chunks) - 1:
            o_ref[0, j * HEAD_DIM:(j + 1) * HEAD_DIM, q0:q0 + q_sub] = (acc / den).astype(BF16)


def _attn(qT, ks, vTs):
    b_, _, rows = qT.shape
    tq = min(Q_TILE, rows)
    key_chunk = min([KEY_CHUNK] + [k.shape[1] for k in ks])
    assert all(k.shape[1] % key_chunk == 0 for k in ks)
    return pl.pallas_call(
        functools.partial(_attn_kernel, n_seg=len(ks), key_chunk=key_chunk),
        out_shape=jax.ShapeDtypeStruct(qT.shape, BF16),
        grid=(b_, rows // tq),
        in_specs=([pl.BlockSpec((1, ATTN_W, tq), lambda b, i: (b, 0, i))]
                  + [pl.BlockSpec((1, k.shape[1], KV_W), lambda b, i: (b, 0, 0)) for k in ks]
                  + [pl.BlockSpec((1, KV_W, v.shape[2]), lambda b, i: (b, 0, 0)) for v in vTs]),
        out_specs=pl.BlockSpec((1, ATTN_W, tq), lambda b, i: (b, 0, i)),
        compiler_params=_params(2),
        name="attn",
    )(qT, *ks, *vTs)


def _mix_kernel(l_ref, x_ref, gate_ref, oT_ref, up_ref, xcs_ref, cl_ref, nsl_ref, fftw_ref, poolw_ref, pscale_ref,
                wout_ref, o_ref, pad_ref, *, seq, tm):
    i = pl.program_id(1)

    @pl.when(i == 0)
    def _():
        zeros = jnp.zeros((POOL_PAD, POOL_W), F32)
        pad_ref[0:POOL_PAD, :] = zeros
        pad_ref[POOL_PAD + seq:seq + 2 * POOL_PAD, :] = zeros
        pad_ref[POOL_PAD:POOL_PAD + seq, :] = up_ref[0]

    n = tm + 2 * POOL_PAD
    r0 = pl.multiple_of(i * tm, tm)
    first_half = lax.broadcasted_iota(jnp.int32, (1, LANES), 1) < POOL_GROUP_W
    pm_tiles = []
    for tile in range(POOL_W // LANES):
        xe = pad_ref[pl.ds(r0, n), tile * LANES:(tile + 1) * LANES]
        hws = POOL_HALF_WINDOWS[2 * tile:2 * tile + 2]
        fwd, w, centred = xe, 1, []
        while w < 2 * hws[-1]:
            fwd = fwd + pltpu.roll(fwd, n - w, axis=0)
            w *= 2
            if w // 2 in hws:
                centred.append(pltpu.roll(fwd, w // 2, axis=0))
        win = jnp.where(first_half, centred[0], centred[1])
        hw = jnp.where(first_half, hws[0], hws[1])
        inner = win * (0.5 / hw.astype(F32)) - xe

        def edge(lo):
            t = lax.broadcasted_iota(jnp.int32, (POOL_PAD, LANES), 0) + (r0 + lo - POOL_PAD)
            cnt = (jnp.minimum(t + hw, seq) - jnp.maximum(t - hw, 0)).astype(F32)
            return win[lo:lo + POOL_PAD] / cnt - xe[lo:lo + POOL_PAD]

        out = jnp.concatenate([edge(POOL_PAD), inner[2 * POOL_PAD:tm], edge(tm)], axis=0)
        pm_tiles.append(out.astype(BF16))
    pm = jnp.concatenate(pm_tiles, axis=1)

    f = (_dot(cl_ref[pl.ds(r0, tm), :], xcs_ref[0, :, 0:FFT_W])
         + _dot(nsl_ref[pl.ds(r0, tm), :], xcs_ref[0, :, FFT_W:2 * FFT_W]))
    yf = _dot(f.astype(BF16), fftw_ref[0])
    yp = _dot(pm, poolw_ref[0]) * pscale_ref[0]
    y = (_dot_tn(oT_ref[0], wout_ref[0, 0:ATTN_W, :])
         + _dot(yp.astype(BF16), wout_ref[0, ATTN_W:ATTN_W + POOL_W, :])
         + _dot(yf.astype(BF16), wout_ref[0, ATTN_W + POOL_W:, :]))
    o_ref[0] = x_ref[0] + gate_ref[0, 0, 0] * y


def _mix(l_arr, x, mod, per_batch, ctx_row, oT, up, xcs, cl, nsl, fftw, poolbd, pscale, wout):
    b_, seq, d = x.shape
    tm = min(ROW_TILE, seq)
    return pl.pallas_call(
        functools.partial(_mix_kernel, seq=seq, tm=tm),
        out_shape=jax.ShapeDtypeStruct(x.shape, F32),
        grid_spec=pltpu.PrefetchScalarGridSpec(
            num_scalar_prefetch=1,
            grid=(b_, seq // tm),
            in_specs=[pl.BlockSpec((1, tm, d), lambda b, i, l: (b, i, 0)),
                      _mod_spec(5, per_batch, ctx_row)(d),
                      pl.BlockSpec((1, ATTN_W, tm), lambda b, i, l: (b, 0, i)),
                      pl.BlockSpec((1, seq, POOL_W), lambda b, i, l: (b, 0, 0)),
                      pl.BlockSpec((1, seq, 2 * FFT_W), lambda b, i, l: (b, 0, 0)),
                      _resident((seq, seq), lambda b, i, l: (0, 0)),
                      _resident((seq, seq), lambda b, i, l: (0, 0)),
                      pl.BlockSpec((1, FFT_W, FFT_W), lambda b, i, l: (l[0], 0, 0)),
                      pl.BlockSpec((1, POOL_W, POOL_W), lambda b, i, l: (l[0], 0, 0)),
                      pl.BlockSpec((1, 1, POOL_W), lambda b, i, l: (l[0], 0, 0)),
                      pl.BlockSpec((1, d, d), lambda b, i, l: (l[0], 0, 0))],
            out_specs=pl.BlockSpec((1, tm, d), lambda b, i, l: (b, i, 0)),
            scratch_shapes=[pltpu.VMEM((seq + 2 * POOL_PAD, POOL_W), F32)]),
        compiler_params=_params(2),
        name="mix",
    )(l_arr, x, mod, oT, up, xcs, cl, nsl, fftw, poolbd, pscale, wout)


def _rope_tables(length):
    t = jnp.arange(length)
    n_freq = HEAD_DIM // 4
    inv_freq = ROPE_THETA ** (-jnp.arange(n_freq, dtype=F32) / n_freq)
    row = (t // GRID_W).astype(F32)
    col = (t % GRID_W).astype(F32)
    ang = jnp.concatenate([inv_freq[:, None] * row[None, :], inv_freq[:, None] * col[None, :]], axis=0)
    return jnp.cos(ang), jnp.sin(ang)


def _dft(n):
    inner = math.gcd(n, 64)
    k = jnp.arange(n)[:, None]

    def table(t):
        ang = ((k * t[None, :]) % n).astype(F32) * (2.0 * math.pi / n)
        return jnp.cos(ang), jnp.sin(ang)

    (ca, sa), (cb, sb) = table(jnp.arange(n // inner) * inner), table(jnp.arange(inner))
    ca, sa, cb, sb = ca[:, :, None], sa[:, :, None], cb[:, None, :], sb[:, None, :]
    scale = n ** -0.5
    return ((ca * cb - sa * sb) * scale).reshape(n, n), ((sa * cb + ca * sb) * scale).reshape(n, n)


def _block_diag(blocks):
    g, w, _ = blocks.shape
    out = jnp.zeros((g * w, g * w), blocks.dtype)
    for i in range(g):
        out = out.at[i * w:(i + 1) * w, i * w:(i + 1) * w].set(blocks[i])
    return out


def kernel(x, c, ctx, c_ctx, ada_w, ada_b, ffn1_w_gu, ffn1_w_down, w_in, q_gain, k_gain, pool_w, pool_scale,
           fft_w, w_out, ffn2_w_gu, ffn2_w_down, final_gain):
    bsz, seq, d = x.shape
    clen = ctx.shape[1]
    depth = ada_w.shape[0]

    rows = -(-(bsz + 1) // 8) * 8
    cc = jnp.zeros((rows, d), F32).at[:bsz].set(c).at[bsz].set(c_ctx)
    mod = _ada(cc, ada_w, ada_b).reshape(depth, N_MOD, rows, 1, d)

    w1gu, w1d = ffn1_w_gu.astype(BF16), ffn1_w_down.astype(BF16)
    w2gu, w2d = ffn2_w_gu.astype(BF16), ffn2_w_down.astype(BF16)
    wqkvT = jnp.swapaxes(w_in[:, :, :QKV_W], 1, 2).astype(BF16)
    wpf = w_in[:, :, QKV_W:].astype(BF16)
    qg = q_gain.reshape(depth, HEAD_DIM, 1)
    kg = k_gain.reshape(depth, HEAD_DIM, 1)
    poolbd = jax.vmap(_block_diag)(pool_w).astype(BF16)
    pscale = pool_scale.reshape(depth, 1, POOL_W)
    fftw = fft_w.astype(BF16)
    wout = w_out.astype(BF16)
    fg = final_gain.reshape(1, d)

    cosT, sinT = _rope_tables(seq)
    ones = jnp.ones((HALF, clen), F32)
    cc_, sc_ = _dft(FFT_GROUP_W)
    dftc = jnp.concatenate([_block_diag(jnp.stack([cc_] * N_FFT_GROUPS)),
                            _block_diag(jnp.stack([sc_] * N_FFT_GROUPS))], axis=1).astype(BF16)
    cl_x, sl_x = _dft(seq)
    cl_c, sl_c = _dft(clen)
    cl_x, nsl_x = cl_x.astype(BF16), (-sl_x).astype(BF16)
    cl_c, nsl_c = cl_c.astype(BF16), (-sl_c).astype(BF16)
    cosT, sinT, dftc, cl_x, nsl_x, cl_c, nsl_c = lax.optimization_barrier(
        (cosT, sinT, dftc, cl_x, nsl_x, cl_c, nsl_c))

    def ffn_x(l_arr, xv, ks, wgu, wd, final=False):
        return _ffn(l_arr, xv, mod, ks, True, bsz, wgu, wd, fg, final)

    def ffn_c(l_arr, cv, ks, wgu, wd):
        flat = cv.reshape(1, bsz * clen, d)
        return _ffn(l_arr, flat, mod, ks, False, bsz, wgu, wd, fg, False).reshape(bsz, clen, d)

    def first_half_step(l_arr, xv, cv):
        return ffn_x(l_arr, xv, (0, 1, 2), w1gu, w1d), ffn_c(l_arr, cv, (0, 1, 2), w1gu, w1d)

    def rest_of_layer(l_arr, xv, cv, last):
        qT, k, vT, up, xcs = _inproj(l_arr, xv, mod, True, bsz, wqkvT, wpf, qg, kg, cosT, sinT, dftc, True)
        qcT, kc, vcT, upc, xcsc = _inproj(l_arr, cv, mod, False, bsz, wqkvT, wpf, qg, kg, ones, ones, dftc, False)
        oT = _attn(qT, [k, kc], [vT, vcT])
        xv = _mix(l_arr, xv, mod, True, bsz, oT, up, xcs, cl_x, nsl_x, fftw, poolbd, pscale, wout)
        xv = ffn_x(l_arr, xv, (6, 7, 8), w2gu, w2d, final=last)
        if not last:
            ocT = _attn(qcT, [kc], [vcT])
            cv = _mix(l_arr, cv, mod, False, bsz, ocT, upc, xcsc, cl_c, nsl_c, fftw, poolbd, pscale, wout)
            cv = ffn_c(l_arr, cv, (6, 7, 8), w2gu, w2d)
        return xv, cv

    def body(carry, l):
        xv, cv = rest_of_layer(l.reshape(1), carry[0], carry[1], False)
        return first_half_step(l.reshape(1) + 1, xv, cv), None

    carry = first_half_step(jnp.zeros((1,), jnp.int32), x, ctx)
    (x, ctx), _ = lax.scan(body, carry, jnp.arange(depth - 1, dtype=jnp.int32))
    x, _ = rest_of_layer(jnp.full((1,), depth - 1, jnp.int32), x, ctx, True)
    return x
```
